```python
import math
import jax, jax.numpy as jnp
from jax import lax
import numpy as np

D_MODEL = 2048
BATCH = 8
SEQ = 4096
DEPTH = 4

MIX_WIDTH = D_MODEL
CONV_WIDTH = D_MODEL // 2
CONV_GROUPS = 8
CONV_K = 3
ATTN_HEADS = 8
ATTN_WIDTH = MIX_WIDTH - CONV_WIDTH
ATTN_VDIM = ATTN_WIDTH // ATTN_HEADS
ATTN_SUBDIM = ATTN_VDIM // 2
ROPE_DIM = ATTN_SUBDIM // 4
ROPE_THETA = 500000.0
Q_BLOCK = 128
IN_COLS = 3 * CONV_WIDTH + 3 * ATTN_WIDTH
PEER_HEADS = 8
PEER_NKEYS = 128
PEER_N = PEER_NKEYS * PEER_NKEYS
PEER_QDIM = 256
PEER_HALF = PEER_QDIM // 2
PEER_TOPK = 16
PEER_CHUNK = 128
NORM_EPS = 1e-6

kernel_name = "hybrid_conv_diffattn_peer_encoder"


def rmsnorm(x, g):
    x32 = x.astype(jnp.float32)
    y = x32 * lax.rsqrt(jnp.mean(x32 * x32, axis=-1, keepdims=True) + NORM_EPS)
    return y.astype(x.dtype) * g


def rotary_tables(positions, dtype):
    inv_freq = ROPE_THETA ** (-jnp.arange(0, ROPE_DIM, 2, dtype=jnp.float32) / ROPE_DIM)
    freqs = positions.astype(jnp.float32)[..., None] * inv_freq
    emb = jnp.concatenate([freqs, freqs], axis=-1)
    return (jnp.cos(emb)[:, :, None, None, :].astype(dtype),
            jnp.sin(emb)[:, :, None, None, :].astype(dtype))


def partial_rope(x, cos, sin):
    xr, xp = x[..., :ROPE_DIM], x[..., ROPE_DIM:]
    x1, x2 = xr[..., :ROPE_DIM // 2], xr[..., ROPE_DIM // 2:]
    rot = jnp.concatenate([-x2, x1], axis=-1)
    return jnp.concatenate([xr * cos + rot * sin, xp], axis=-1)


def short_conv_group(gb, gc, hv, conv_w):
    z = gc * hv
    zp = jnp.pad(z, ((0, 0), (1, 1), (0, 0)))
    y = conv_w[0] * zp[:, :-2] + conv_w[1] * zp[:, 1:-1] + conv_w[2] * zp[:, 2:]
    return gb * y


def diff_attention_group(q, k, v, cos, sin, lq1, lk1, lq2, lk2, subln_g, lam_init):
    bsz, seq, _ = q.shape
    q = partial_rope(q.reshape(bsz, seq, ATTN_HEADS, 2, ATTN_SUBDIM), cos, sin)
    k = partial_rope(k.reshape(bsz, seq, ATTN_HEADS, 2, ATTN_SUBDIM), cos, sin)
    q = q.transpose(0, 2, 3, 1, 4)
    k = k.transpose(0, 2, 3, 1, 4)
    v = v.reshape(bsz, seq, ATTN_HEADS, ATTN_VDIM).transpose(0, 2, 1, 3)
    lam = (jnp.exp(jnp.sum(lq1.astype(jnp.float32) * lk1.astype(jnp.float32)))
           - jnp.exp(jnp.sum(lq2.astype(jnp.float32) * lk2.astype(jnp.float32)))
           + lam_init)
    scale = ATTN_SUBDIM ** -0.5
    nb = seq // Q_BLOCK
    qb = jnp.moveaxis(q.reshape(bsz, ATTN_HEADS, 2, nb, Q_BLOCK, ATTN_SUBDIM), 3, 0)

    def block(qi):
        s = jnp.einsum('bhcqd,bhckd->bhcqk', qi, k).astype(jnp.float32) * scale
        p = jax.nn.softmax(s, axis=-1)
        a = (p[:, :, 0] - lam * p[:, :, 1]).astype(v.dtype)
        return jnp.einsum('bhqk,bhkd->bhqd', a, v)

    o = lax.map(block, qb)
    o = o.transpose(1, 0, 3, 2, 4).reshape(bsz, seq, ATTN_HEADS, ATTN_VDIM)
    o = rmsnorm(o, subln_g) * (1.0 - lam_init)
    return o.reshape(bsz, seq, ATTN_WIDTH)


def hybrid_mixer(h, w_in, conv_w, lq1, lk1, lq2, lk2, subln_g, w_out, cos, sin, lam_init):
    z = h @ w_in
    c0, c1, c2 = CONV_WIDTH, 2 * CONV_WIDTH, 3 * CONV_WIDTH
    gb, gc, hv, q, k, v = jnp.split(z, [c0, c1, c2, c2 + ATTN_WIDTH, c2 + 2 * ATTN_WIDTH], axis=-1)
    conv_out = short_conv_group(gb, gc, hv, conv_w)
    attn_out = diff_attention_group(q, k, v, cos, sin, lq1, lk1, lq2, lk2, subln_g, lam_init)
    return jnp.concatenate([conv_out, attn_out], axis=-1) @ w_out


def peer_ffn(h, wq, keys, u, v):
    bsz, seq, d = h.shape
    q = (h @ wq).reshape(bsz, seq, PEER_HEADS, 2, PEER_HALF)
    s = jnp.einsum('bshcd,hcnd->bshcn', q, keys)
    st, it = lax.top_k(s, PEER_TOPK)
    cand = (st[..., 0, :, None] + st[..., 1, None, :]).reshape(bsz, seq, PEER_HEADS, PEER_TOPK * PEER_TOPK)
    cidx = (it[..., 0, :, None] * PEER_NKEYS + it[..., 1, None, :]).reshape(bsz, seq, PEER_HEADS, PEER_TOPK * PEER_TOPK)
    sc, pos = lax.top_k(cand, PEER_TOPK)
    experts = jnp.take_along_axis(cidx, pos, axis=-1)
    g = jax.nn.softmax(sc.astype(jnp.float32), axis=-1).astype(h.dtype)
    n_chunks = (bsz * seq) // PEER_CHUNK
    xs = h.reshape(n_chunks, PEER_CHUNK, d)
    es = experts.reshape(n_chunks, PEER_CHUNK, PEER_HEADS * PEER_TOPK)
    gs = g.reshape(n_chunks, PEER_CHUNK, PEER_HEADS * PEER_TOPK)

    def chunk(args):
        xc, ec, gcnk = args
        a = jnp.einsum('tkd,td->tk', u[ec], xc)
        a = jax.nn.gelu(a, approximate=False) * gcnk
        return jnp.einsum('tk,tkd->td', a, v[ec])

    out = lax.map(chunk, (xs, es, gs))
    return out.reshape(bsz, seq, d)


def setup_inputs(seed: int = 0) -> dict:
    key = jax.random.key(seed)
    ks = jax.random.split(key, 20)
    D = D_MODEL
    f32 = jnp.float32
    x = jax.random.normal(ks[0], (BATCH, SEQ, D), f32)
    c = jax.random.normal(ks[1], (BATCH, D), f32)
    offset = jax.random.randint(ks[2], (BATCH, 1), 0, 1024, dtype=jnp.int32)
    positions = offset + jnp.arange(SEQ, dtype=jnp.int32)[None, :]
    ada_w = jax.random.normal(ks[3], (DEPTH, D, 6 * D), f32) * (0.5 * D ** -0.5)
    ada_b = jax.random.normal(ks[4], (DEPTH, 6 * D), f32) * 0.02
    norm1_g = 1.0 + 0.05 * jax.random.normal(ks[5], (DEPTH, D), f32)
    w_in = jax.random.normal(ks[6], (DEPTH, D, IN_COLS), f32) * D ** -0.5
    conv_w = jax.random.normal(ks[7], (DEPTH, CONV_K, CONV_WIDTH), f32) * CONV_K ** -0.5
    lambda_q1 = jax.random.normal(ks[8], (DEPTH, ATTN_SUBDIM), f32) * 0.1
    lambda_k1 = jax.random.normal(ks[9], (DEPTH, ATTN_SUBDIM), f32) * 0.1
    lambda_q2 = jax.random.normal(ks[10], (DEPTH, ATTN_SUBDIM), f32) * 0.1
    lambda_k2 = jax.random.normal(ks[11], (DEPTH, ATTN_SUBDIM), f32) * 0.1
    subln_g = 1.0 + 0.05 * jax.random.normal(ks[12], (DEPTH, ATTN_VDIM), f32)
    w_out = jax.random.normal(ks[13], (DEPTH, MIX_WIDTH, D), f32) * MIX_WIDTH ** -0.5
    norm2_g = 1.0 + 0.05 * jax.random.normal(ks[14], (DEPTH, D), f32)
    peer_wq = jax.random.normal(ks[15], (DEPTH, D, PEER_HEADS * PEER_QDIM), f32) * D ** -0.5
    peer_keys = jax.random.normal(ks[16], (DEPTH, PEER_HEADS, 2, PEER_NKEYS, PEER_HALF), f32) * PEER_HALF ** -0.5
    peer_u = jax.random.normal(ks[17], (DEPTH, PEER_N, D), f32) * D ** -0.5
    peer_v = jax.random.normal(ks[18], (DEPTH, PEER_N, D), f32) * 0.5
    final_g = 1.0 + 0.05 * jax.random.normal(ks[19], (D,), f32)
    return {"x": x, "c": c, "positions": positions, "ada_w": ada_w, "ada_b": ada_b,
            "norm1_g": norm1_g, "w_in": w_in, "conv_w": conv_w,
            "lambda_q1": lambda_q1, "lambda_k1": lambda_k1, "lambda_q2": lambda_q2, "lambda_k2": lambda_k2,
            "subln_g": subln_g, "w_out": w_out, "norm2_g": norm2_g, "peer_wq": peer_wq,
            "peer_keys": peer_keys, "peer_u": peer_u, "peer_v": peer_v, "final_g": final_g}


def reference(x, c, positions, ada_w, ada_b, norm1_g, w_in, conv_w, lambda_q1, lambda_k1,
              lambda_q2, lambda_k2, subln_g, w_out, norm2_g, peer_wq, peer_keys, peer_u,
              peer_v, final_g):
    cos, sin = rotary_tables(positions, x.dtype)
    c_act = jax.nn.silu(c)
    for l in range(DEPTH):
        lam_init = 0.8 - 0.6 * math.exp(-0.3 * l)
        mod = c_act @ ada_w[l] + ada_b[l]
        sh1, sc1, g1, sh2, sc2, g2 = [m[:, None, :] for m in jnp.split(mod, 6, axis=-1)]
        h = rmsnorm(x, norm1_g[l]) * (1.0 + sc1) + sh1
        x = x + g1 * hybrid_mixer(h, w_in[l], conv_w[l], lambda_q1[l], lambda_k1[l],
                                  lambda_q2[l], lambda_k2[l], subln_g[l], w_out[l], cos, sin, lam_init)
        h = rmsnorm(x, norm2_g[l]) * (1.0 + sc2) + sh2
        x = x + g2 * peer_ffn(h, peer_wq[l], peer_keys[l], peer_u[l], peer_v[l])
    return rmsnorm(x, final_g)
```

```python
import functools
import math

import jax
import jax.numpy as jnp
from jax import lax
from jax.experimental import pallas as pl
from jax.experimental.pallas import tpu as pltpu

F32 = jnp.float32
BF16 = jnp.bfloat16

NORM_EPS = 1e-6
ROPE_THETA = 500000.0
ATTN_HEADS = 8
ROPE_DIM = 16
PEER_HEADS = 8
PEER_TOPK = 16
CONV_K = 3

LANES = 128
SUBLANES = 8
VMEM_LIMIT = 56 * 1024 * 1024


def _tile(n, pref):
    return pref if n % pref == 0 else n


def _params(sem):
    return pltpu.CompilerParams(dimension_semantics=sem, vmem_limit_bytes=VMEM_LIMIT)


def _ada_kernel(cb_ref, w_ref, b_ref, o_ref):
    d_model = w_ref.shape[1]
    tn = w_ref.shape[2]
    bsz = cb_ref.shape[1]
    groups = tn // LANES

    def body(d8, acc):
        acc = list(acc)
        for r in range(SUBLANES):
            d = d8 * SUBLANES + r
            cv = cb_ref[d]
            cv = cv / (1.0 + jnp.exp(-cv))
            wrow = w_ref[0, pl.ds(d, 1), :]
            for g in range(groups):
                acc[g] = acc[g] + cv * wrow[:, g * LANES:(g + 1) * LANES]
        return tuple(acc)

    acc0 = tuple(jnp.zeros((bsz, LANES), F32) for _ in range(groups))
    acc = lax.fori_loop(0, d_model // SUBLANES, body, acc0)
    for g in range(groups):
        o_ref[0, :, g * LANES:(g + 1) * LANES] = acc[g] + b_ref[0, :, g * LANES:(g + 1) * LANES]


def _ada_modulation(c, ada_w, ada_b):
    depth, d_model, n_out = ada_w.shape
    bsz = c.shape[0]
    tn = _tile(n_out, 1024)
    cb = jnp.broadcast_to(c.T[:, :, None], (d_model, bsz, LANES))
    return pl.pallas_call(
        _ada_kernel,
        grid=(depth, n_out // tn),
        in_specs=[
            pl.BlockSpec((d_model, bsz, LANES), lambda l, j: (0, 0, 0), pipeline_mode=pl.Buffered(1)),
            pl.BlockSpec((1, d_model, tn), lambda l, j: (l, 0, j)),
            pl.BlockSpec((1, 1, tn), lambda l, j: (l, 0, j)),
        ],
        out_specs=pl.BlockSpec((1, bsz, tn), lambda l, j: (l, 0, j)),
        out_shape=jax.ShapeDtypeStruct((depth, bsz, n_out), F32),
        compiler_params=_params(("arbitrary", "arbitrary")),
        name="ada_modulation",
    )(cb, ada_w, ada_b.reshape(depth, 1, n_out))


def _rope_table_kernel(pos_ref, freq_ref, c_ref, s1_ref, s2_ref):
    ang = pos_ref[...].astype(F32) * freq_ref[...]
    lane = lax.broadcasted_iota(jnp.int32, ang.shape, 1) % (LANES // 2)
    half = ROPE_DIM // 2
    cosv = jnp.cos(ang)
    sinv = jnp.sin(ang)
    c_ref[...] = jnp.where(lane < ROPE_DIM, cosv, 1.0)
    s1_ref[...] = jnp.where((lane >= half) & (lane < ROPE_DIM), sinv, 0.0)
    s2_ref[...] = jnp.where(lane < half, -sinv, 0.0)


def _rope_tables(positions):
    n_tok = positions.size
    half = ROPE_DIM // 2
    inv_freq = ROPE_THETA ** (-jnp.arange(0, ROPE_DIM, 2, dtype=F32) / ROPE_DIM)
    sub = jnp.concatenate([inv_freq, inv_freq, jnp.zeros((LANES // 2 - 2 * half,), F32)])
    freq_row = jnp.concatenate([sub, sub]).reshape(1, LANES)
    tm = _tile(n_tok, 2048)
    tab = jax.ShapeDtypeStruct((n_tok, LANES), F32)
    return pl.pallas_call(
        _rope_table_kernel,
        grid=(n_tok // tm,),
        in_specs=[pl.BlockSpec((tm, 1), lambda i: (i, 0)), pl.BlockSpec((1, LANES), lambda i: (0, 0))],
        out_specs=[pl.BlockSpec((tm, LANES), lambda i: (i, 0))] * 3,
        out_shape=[tab, tab, tab],
        compiler_params=_params(("arbitrary",)),
        name="rope_tables",
    )(positions.reshape(n_tok, 1), freq_row)


def _modulated_norm(x, g, sc, sh):
    ms = jnp.mean(x * x, axis=-1, keepdims=True)
    return (x * lax.rsqrt(ms + NORM_EPS)) * g * (1.0 + sc) + sh


def _mix_in_kernel(has_prev, rope_lo, rope_mid, rope_hi, q_scale, *refs):
    if has_prev:
        (x_ref, yt_ref, g2_ref, g_ref, sc_ref, sh_ref, w_ref, rc_ref, rs1_ref, rs2_ref,
         z_ref, xo_ref, h_scr) = refs
    else:
        (x_ref, g_ref, sc_ref, sh_ref, w_ref, rc_ref, rs1_ref, rs2_ref, z_ref, h_scr) = refs
    j = pl.program_id(1)

    @pl.when(j == 0)
    def _():
        x = x_ref[...]
        if has_prev:
            x = x + g2_ref[0] * yt_ref[...].T
            xo_ref[...] = x
        h_scr[...] = _modulated_norm(x, g_ref[...], sc_ref[0], sh_ref[0]).astype(BF16)

    z = jnp.dot(h_scr[...], w_ref[...], preferred_element_type=F32)
    is_rope = (j >= rope_lo) & (j < rope_hi)

    @pl.when(jnp.logical_not(is_rope))
    def _():
        z_ref[...] = z.astype(BF16)

    @pl.when(is_rope)
    def _():
        reps = z.shape[1] // LANES
        scale = jnp.where(j < rope_mid, q_scale, 1.0).astype(F32)
        zr_prev = pltpu.roll(z, ROPE_DIM // 2, axis=1)
        zr_next = pltpu.roll(z, z.shape[1] - ROPE_DIM // 2, axis=1)
        for r in range(reps):
            sl = slice(r * LANES, (r + 1) * LANES)
            out = z[:, sl] * rc_ref[...] + zr_prev[:, sl] * rs1_ref[...] + zr_next[:, sl] * rs2_ref[...]
            z_ref[:, sl] = (out * scale).astype(BF16)


def _mix_in(x, prev, norm_g, sc, sh, w_bf16, rope, seq, q_cols, k_cols, v_cols, q_scale):
    n_tok, d_model = x.shape
    n_out = w_bf16.shape[1]
    tm = _tile(seq, 512)
    tn = _tile(q_cols, 1024)
    assert q_cols % tn == 0 and k_cols % tn == 0 and v_cols % tn == 0
    tiles_per_seq = seq // tm
    has_prev = prev is not None
    row = lambda i, j: (i, 0)
    per_batch = lambda i, j: (i // tiles_per_seq, 0, 0)
    fixed = lambda i, j: (0, 0)
    in_specs = [pl.BlockSpec((tm, d_model), row)]
    args = [x]
    if has_prev:
        yt, g2 = prev
        in_specs += [pl.BlockSpec((d_model, tm), lambda i, j: (0, i)), pl.BlockSpec((1, 1, d_model), per_batch)]
        args += [yt, g2]
    in_specs += [
        pl.BlockSpec((1, d_model), fixed),
        pl.BlockSpec((1, 1, d_model), per_batch),
        pl.BlockSpec((1, 1, d_model), per_batch),
        pl.BlockSpec((d_model, tn), lambda i, j: (0, j)),
        pl.BlockSpec((tm, LANES), row),
        pl.BlockSpec((tm, LANES), row),
        pl.BlockSpec((tm, LANES), row),
    ]
    args += [norm_g.reshape(1, d_model), sc, sh, w_bf16, *rope]
    out_specs = [pl.BlockSpec((tm, tn), lambda i, j: (i, j))]
    out_shape = [jax.ShapeDtypeStruct((n_tok, n_out), BF16)]
    if has_prev:
        out_specs.append(pl.BlockSpec((tm, d_model), row))
        out_shape.append(jax.ShapeDtypeStruct((n_tok, d_model), F32))
    kern = functools.partial(_mix_in_kernel, has_prev, q_cols // tn, k_cols // tn, v_cols // tn, q_scale)
    outs = pl.pallas_call(
        kern,
        grid=(n_tok // tm, n_out // tn),
        in_specs=in_specs,
        out_specs=out_specs,
        out_shape=out_shape,
        scratch_shapes=[pltpu.VMEM((tm, d_model), BF16)],
        compiler_params=_params(("arbitrary", "arbitrary")),
        name="mix_in",
    )(*args)
    return (outs[0], outs[1]) if has_prev else (outs[0], x)


def _attention_kernel(lam_init, q_ref, k_ref, v_ref, lam_ref, g_ref, o_ref):
    tq = q_ref.shape[0]
    q = q_ref[...]
    lane = lax.broadcasted_iota(jnp.int32, q.shape, 1)
    zero = jnp.zeros_like(q)
    qq = jnp.concatenate([jnp.where(lane < LANES // 2, q, zero), jnp.where(lane >= LANES // 2, q, zero)], axis=0)
    s = lax.dot_general(qq, k_ref[...], (((1,), (1,)), ((), ())), preferred_element_type=F32)
    m = jnp.max(s, axis=1, keepdims=True)
    p = jnp.exp(s - m)
    inv_l = 1.0 / jnp.sum(p, axis=1, keepdims=True)
    lv = lam_ref[...]
    lam = (jnp.exp(jnp.sum(lv[0:1] * lv[1:2], axis=1, keepdims=True))
           - jnp.exp(jnp.sum(lv[2:3] * lv[3:4], axis=1, keepdims=True)) + lam_init)
    a = (p[:tq] * inv_l[:tq] - p[tq:] * (lam * inv_l[tq:])).astype(BF16)
    o = jnp.dot(a, v_ref[...], preferred_element_type=F32)
    ms = jnp.mean(o * o, axis=-1, keepdims=True)
    o_ref[...] = ((o * lax.rsqrt(ms + NORM_EPS)) * g_ref[...] * (1.0 - lam_init)).astype(BF16)


def _attention(z, lam_vecs, subln_g, bsz, seq, q_cols, k_cols, v_cols, lam_init):
    n_tok = z.shape[0]
    hd = LANES
    tq = _tile(seq, 256)
    nq = seq // tq
    qb, kb, vb = q_cols // hd, k_cols // hd, v_cols // hd
    return pl.pallas_call(
        functools.partial(_attention_kernel, lam_init),
        grid=(bsz, ATTN_HEADS, nq),
        in_specs=[
            pl.BlockSpec((tq, hd), lambda b, h, i: (b * nq + i, qb + h)),
            pl.BlockSpec((seq, hd), lambda b, h, i: (b, kb + h)),
            pl.BlockSpec((seq, hd), lambda b, h, i: (b, vb + h)),
            pl.BlockSpec(lam_vecs.shape, lambda b, h, i: (0, 0)),
            pl.BlockSpec((1, hd), lambda b, h, i: (0, 0)),
        ],
        out_specs=pl.BlockSpec((tq, hd), lambda b, h, i: (b * nq + i, h)),
        out_shape=jax.ShapeDtypeStruct((n_tok, ATTN_HEADS * hd), BF16),
        compiler_params=_params(("arbitrary", "arbitrary", "arbitrary")),
        name="diff_attention",
    )(z, z, z, lam_vecs, subln_g.reshape(1, hd))


def _mix_out_kernel(tiles_per_seq, gb_ref, gc_ref, hv_ref, gcp_ref, hvp_ref, gcn_ref, hvn_ref, at_ref,
                    cw_ref, w_ref, x_ref, g1_ref, o_ref):
    i = pl.program_id(0)
    tm = gc_ref.shape[0]
    cw = gc_ref.shape[1]
    zc = gc_ref[...].astype(F32) * hv_ref[...].astype(F32)
    first = (i % tiles_per_seq) == 0
    last = (i % tiles_per_seq) == tiles_per_seq - 1
    prev_row = gcp_ref[SUBLANES - 1:SUBLANES, :].astype(F32) * hvp_ref[SUBLANES - 1:SUBLANES, :].astype(F32)
    next_row = gcn_ref[0:1, :].astype(F32) * hvn_ref[0:1, :].astype(F32)
    prev_row = jnp.where(first, 0.0, prev_row)
    next_row = jnp.where(last, 0.0, next_row)
    rows = lax.broadcasted_iota(jnp.int32, zc.shape, 0)
    z_prev = jnp.where(rows == 0, prev_row, pltpu.roll(zc, 1, axis=0))
    z_next = jnp.where(rows == tm - 1, next_row, pltpu.roll(zc, tm - 1, axis=0))
    y = cw_ref[0:1, :] * z_prev + cw_ref[1:2, :] * zc + cw_ref[2:3, :] * z_next
    conv = (gb_ref[...].astype(F32) * y).astype(BF16)
    acc = jnp.dot(conv, w_ref[:cw, :], preferred_element_type=F32)
    acc = acc + jnp.dot(at_ref[...], w_ref[cw:, :], preferred_element_type=F32)
    o_ref[...] = x_ref[...] + g1_ref[0] * acc


def _mix_out(z, attn, conv_w, w_bf16, x, g1, seq, conv_cols):
    n_tok, d_model = x.shape
    cw = conv_cols
    aw = attn.shape[1]
    tm = _tile(seq, 512)
    tiles_per_seq = seq // tm
    hb = tm // SUBLANES
    last_hb = n_tok // SUBLANES - 1
    row = lambda i: (i, 0)
    prev_blk = lambda i: (jnp.maximum(i * hb - 1, 0), 1)
    prev_blk2 = lambda i: (jnp.maximum(i * hb - 1, 0), 2)
    next_blk = lambda i: (jnp.minimum((i + 1) * hb, last_hb), 1)
    next_blk2 = lambda i: (jnp.minimum((i + 1) * hb, last_hb), 2)
    return pl.pallas_call(
        functools.partial(_mix_out_kernel, tiles_per_seq),
        grid=(n_tok // tm,),
        in_specs=[
            pl.BlockSpec((tm, cw), lambda i: (i, 0)),
            pl.BlockSpec((tm, cw), lambda i: (i, 1)),
            pl.BlockSpec((tm, cw), lambda i: (i, 2)),
            pl.BlockSpec((SUBLANES, cw), prev_blk),
            pl.BlockSpec((SUBLANES, cw), prev_blk2),
            pl.BlockSpec((SUBLANES, cw), next_blk),
            pl.BlockSpec((SUBLANES, cw), next_blk2),
            pl.BlockSpec((tm, aw), row),
            pl.BlockSpec((CONV_K, cw), lambda i: (0, 0)),
            pl.BlockSpec((cw + aw, d_model), lambda i: (0, 0), pipeline_mode=pl.Buffered(1)),
            pl.BlockSpec((tm, d_model), row),
            pl.BlockSpec((1, 1, d_model), lambda i: (i // tiles_per_seq, 0, 0)),
        ],
        out_specs=pl.BlockSpec((tm, d_model), row),
        out_shape=jax.ShapeDtypeStruct((n_tok, d_model), F32),
        compiler_params=_params(("arbitrary",)),
        name="mix_out",
    )(z, z, z, z, z, z, z, attn, conv_w, w_bf16, x, g1)


def _split_bf16(a):
    hi = a.astype(BF16)
    lo = (a - hi.astype(F32)).astype(BF16)
    return hi, lo


def _dot3(a_hi, a_lo, b_hi, b_lo):
    d = functools.partial(jnp.dot, preferred_element_type=F32)
    return d(a_hi, b_hi) + (d(a_hi, b_lo) + d(a_lo, b_hi))


def _top_values(s, count):
    vals = []
    work = s
    for k in range(count):
        m = jnp.max(work, axis=0, keepdims=True)
        vals.append(m)
        if k + 1 < count:
            work = jnp.where(work == m, -jnp.inf, work)
    return vals


def _stack_rows(rows):
    n = len(rows)
    idx = lax.broadcasted_iota(jnp.int32, (n, rows[0].shape[1]), 0)
    out = jnp.broadcast_to(rows[0], idx.shape)
    for k in range(1, n):
        out = jnp.where(idx == k, rows[k], out)
    return out


def _peer_score_kernel(x_ref, g_ref, sc_ref, sh_ref, wqh_ref, wql_ref, kh_ref, kl_ref,
                       ht_ref, t1_ref, s2_ref, tau_ref):
    nk = kh_ref.shape[2]
    h = _modulated_norm(x_ref[...], g_ref[...], sc_ref[0], sh_ref[0])
    ht = h.T
    ht_hi, ht_lo = _split_bf16(ht)
    ht_ref[...] = ht_hi
    qt = _dot3(wqh_ref[...], wql_ref[...], ht_hi, ht_lo)
    half = kh_ref.shape[3]
    k = PEER_TOPK
    for hd in range(PEER_HEADS):
        st = []
        for c in range(2):
            r0 = (hd * 2 + c) * half
            q_hi, q_lo = _split_bf16(qt[r0:r0 + half, :])
            st.append(_dot3(kh_ref[hd, c], kl_ref[hd, c], q_hi, q_lo))
        a = _top_values(st[0], k)
        b = _top_values(st[1], k)
        b_all = _stack_rows(b)
        a_tail = _stack_rows(a[k // 2:])
        cand = [a[0] + b_all]
        cand += [a[p] + b_all[:k // 2] for p in range(1, k // 2)]
        cand.append(a_tail + b[0])
        cand = jnp.concatenate(cand, axis=0)
        top = _top_values(cand, k + 1)
        tau = 0.5 * (top[k - 1] + top[k])
        z = jnp.sum(jnp.where(cand > tau, jnp.exp(cand - top[0]), 0.0), axis=0, keepdims=True)
        off = top[0] + jnp.log(z)
        t1_ref[hd * nk:(hd + 1) * nk, :] = st[0] - off
        s2_ref[hd * nk:(hd + 1) * nk, :] = st[1]
        tau_ref[hd:hd + 1, :] = tau - off


def _peer_score(x, norm_g, sc, sh, wqt_hi, wqt_lo, keys_hi, keys_lo, seq):
    n_tok, d_model = x.shape
    nk = keys_hi.shape[2]
    tm = _tile(seq, 256)
    tiles_per_seq = seq // tm
    per_batch = lambda i: (i // tiles_per_seq, 0, 0)
    once = pl.Buffered(1)
    col = lambda i: (0, i)
    return pl.pallas_call(
        _peer_score_kernel,
        grid=(n_tok // tm,),
        in_specs=[
            pl.BlockSpec((tm, d_model), lambda i: (i, 0)),
            pl.BlockSpec((1, d_model), lambda i: (0, 0)),
            pl.BlockSpec((1, 1, d_model), per_batch),
            pl.BlockSpec((1, 1, d_model), per_batch),
            pl.BlockSpec(wqt_hi.shape, lambda i: (0, 0), pipeline_mode=once),
            pl.BlockSpec(wqt_lo.shape, lambda i: (0, 0), pipeline_mode=once),
            pl.BlockSpec(keys_hi.shape, lambda i: (0, 0, 0, 0), pipeline_mode=once),
            pl.BlockSpec(keys_lo.shape, lambda i: (0, 0, 0, 0), pipeline_mode=once),
        ],
        out_specs=[
            pl.BlockSpec((d_model, tm), col),
            pl.BlockSpec((PEER_HEADS * nk, tm), col),
            pl.BlockSpec((PEER_HEADS * nk, tm), col),
            pl.BlockSpec((PEER_HEADS, tm), col),
        ],
        out_shape=[
            jax.ShapeDtypeStruct((d_model, n_tok), BF16),
            jax.ShapeDtypeStruct((PEER_HEADS * nk, n_tok), F32),
            jax.ShapeDtypeStruct((PEER_HEADS * nk, n_tok), F32),
            jax.ShapeDtypeStruct((PEER_HEADS, n_tok), F32),
        ],
        compiler_params=_params(("arbitrary",)),
        name="peer_score",
    )(x, norm_g.reshape(1, d_model), sc, sh, wqt_hi, wqt_lo, keys_hi, keys_lo)


GATE_ROWS = 32
GATE_COLS = 256


def _peer_main_kernel(nk, u_ref, vt_ref, ht_ref, t1_ref, s2_ref, tau_ref, o_ref, a_scr, ag_scr):
    e = pl.program_id(1)
    eb, tm = a_scr.shape
    nb = eb // nk

    @pl.when(e == 0)
    def _():
        o_ref[...] = jnp.zeros_like(o_ref)

    a_scr[...] = jnp.dot(u_ref[...], ht_ref[...], preferred_element_type=F32)

    def col_chunk(cc, carry):
        c0 = pl.multiple_of(cc * GATE_COLS, GATE_COLS)
        cols = pl.ds(c0, GATE_COLS)
        taus = [tau_ref[hd:hd + 1, cols] for hd in range(PEER_HEADS)]
        for ii in range(nb):
            i = e * nb + ii
            t1s = [t1_ref[pl.ds(hd * nk + i, 1), cols] for hd in range(PEER_HEADS)]
            for r0 in range(0, nk, GATE_ROWS):
                gate = jnp.zeros((GATE_ROWS, GATE_COLS), F32)
                for hd in range(PEER_HEADS):
                    sm = t1s[hd] + s2_ref[hd * nk + r0:hd * nk + r0 + GATE_ROWS, cols]
                    gate = gate + jnp.where(sm > taus[hd], jnp.exp(sm), 0.0)
                rows = slice(ii * nk + r0, ii * nk + r0 + GATE_ROWS)
                act = a_scr[rows, cols]
                gel = 0.5 * act * (1.0 + lax.erf(act * (1.0 / math.sqrt(2.0))))
                ag_scr[rows, cols] = (gel * gate).astype(BF16)
        return carry

    lax.fori_loop(0, tm // GATE_COLS, col_chunk, 0)
    o_ref[...] += jnp.dot(vt_ref[...], ag_scr[...], preferred_element_type=F32)


def _peer_main(u_bf16, vt_bf16, ht, t1, s2, tau, nk):
    n_exp, d_model = u_bf16.shape
    n_tok = ht.shape[1]
    tm = _tile(n_tok, 1024)
    eb = _tile(n_exp, 512)
    once = pl.Buffered(1)
    col = lambda i, e: (0, i)
    return pl.pallas_call(
        functools.partial(_peer_main_kernel, nk),
        grid=(n_tok // tm, n_exp // eb),
        in_specs=[
            pl.BlockSpec((eb, d_model), lambda i, e: (e, 0)),
            pl.BlockSpec((d_model, eb), lambda i, e: (0, e)),
            pl.BlockSpec((d_model, tm), col, pipeline_mode=once),
            pl.BlockSpec((t1.shape[0], tm), col, pipeline_mode=once),
            pl.BlockSpec((s2.shape[0], tm), col, pipeline_mode=once),
            pl.BlockSpec((tau.shape[0], tm), col, pipeline_mode=once),
        ],
        out_specs=pl.BlockSpec((d_model, tm), col),
        out_shape=jax.ShapeDtypeStruct((d_model, n_tok), F32),
        scratch_shapes=[pltpu.VMEM((eb, tm), F32), pltpu.VMEM((eb, tm), BF16)],
        compiler_params=_params(("arbitrary", "arbitrary")),
        name="peer_main",
    )(u_bf16, vt_bf16, ht, t1, s2, tau)


def _final_kernel(x_ref, yt_ref, g2_ref, g_ref, o_ref):
    x = x_ref[...] + g2_ref[0] * yt_ref[...].T
    ms = jnp.mean(x * x, axis=-1, keepdims=True)
    o_ref[...] = (x * lax.rsqrt(ms + NORM_EPS)) * g_ref[...]


def _final(x, yt, g2, final_g, seq):
    n_tok, d_model = x.shape
    tm = _tile(seq, 512)
    tiles_per_seq = seq // tm
    return pl.pallas_call(
        _final_kernel,
        grid=(n_tok // tm,),
        in_specs=[
            pl.BlockSpec((tm, d_model), lambda i: (i, 0)),
            pl.BlockSpec((d_model, tm), lambda i: (0, i)),
            pl.BlockSpec((1, 1, d_model), lambda i: (i // tiles_per_seq, 0, 0)),
            pl.BlockSpec((1, d_model), lambda i: (0, 0)),
        ],
        out_specs=pl.BlockSpec((tm, d_model), lambda i: (i, 0)),
        out_shape=jax.ShapeDtypeStruct((n_tok, d_model), F32),
        compiler_params=_params(("arbitrary",)),
        name="final_norm",
    )(x, yt, g2, final_g.reshape(1, d_model))


def kernel(x, c, positions, ada_w, ada_b, norm1_g, w_in, conv_w, lambda_q1, lambda_k1, lambda_q2, lambda_k2,
           subln_g, w_out, norm2_g, peer_wq, peer_keys, peer_u, peer_v, final_g):
    bsz, seq, d_model = x.shape
    depth = ada_w.shape[0]
    n_tok = bsz * seq
    conv_cols = conv_w.shape[2]
    attn_cols = w_out.shape[1] - conv_cols
    q_cols = 3 * conv_cols
    k_cols = q_cols + attn_cols
    v_cols = k_cols + attn_cols
    nk = peer_keys.shape[3]
    sub_dim = attn_cols // ATTN_HEADS // 2
    q_scale = sub_dim ** -0.5

    rope = _rope_tables(positions)
    mod = _ada_modulation(c, ada_w, ada_b)
    xf = x.reshape(n_tok, d_model)
    prev = None
    for l in range(depth):
        lam_init = 0.8 - 0.6 * math.exp(-0.3 * l)
        sh1, sc1, g1, sh2, sc2, g2 = [m.reshape(bsz, 1, d_model) for m in jnp.split(mod[l], 6, axis=-1)]
        z, xf = _mix_in(xf, prev, norm1_g[l], sc1, sh1, w_in[l].astype(BF16), rope, seq,
                        q_cols, k_cols, v_cols, q_scale)
        lam_vecs = jnp.stack([lambda_q1[l], lambda_k1[l], lambda_q2[l], lambda_k2[l]])
        attn = _attention(z, lam_vecs, subln_g[l], bsz, seq, q_cols, k_cols, v_cols, lam_init)
        xf = _mix_out(z, attn, conv_w[l], w_out[l].astype(BF16), xf, g1, seq, conv_cols)
        wqt_hi, wqt_lo = _split_bf16(peer_wq[l].T)
        keys_hi, keys_lo = _split_bf16(peer_keys[l])
        ht, t1, s2, tau = _peer_score(xf, norm2_g[l], sc2, sh2, wqt_hi, wqt_lo, keys_hi, keys_lo, seq)
        yt = _peer_main(peer_u[l].astype(BF16), peer_v[l].T.astype(BF16), ht, t1, s2, tau, nk)
        prev = (yt, g2)
    out = _final(xf, prev[0], prev[1], final_g, seq)
    return out.reshape(bsz, seq, d_model)
```

```python
import functools
import math

import jax
import jax.numpy as jnp
from jax import lax
from jax.experimental import pallas as pl
from jax.experimental.pallas import tpu as pltpu

F32 = jnp.float32
BF16 = jnp.bfloat16

NORM_EPS = 1e-6
ROPE_THETA = 500000.0
ATTN_HEADS = 8
ROPE_DIM = 16
PEER_HEADS = 8
PEER_TOPK = 16
CONV_K = 3

MXU_DIM = 256
LANES = 128
SUBLANES = 8
VMEM_LIMIT = 56 * 1024 * 1024


def _tile(n, pref):
    return pref if n % pref == 0 else n


def _params(sem):
    return pltpu.CompilerParams(dimension_semantics=sem, vmem_limit_bytes=VMEM_LIMIT)


def _ada_kernel(cb_ref, w_ref, b_ref, o_ref):
    d_model = w_ref.shape[1]
    tn = w_ref.shape[2]
    bsz = cb_ref.shape[1]
    groups = tn // LANES

    def body(d8, acc):
        acc = list(acc)
        for r in range(SUBLANES):
            d = d8 * SUBLANES + r
            cv = cb_ref[d]
            cv = cv / (1.0 + jnp.exp(-cv))
            wrow = w_ref[0, pl.ds(d, 1), :]
            for g in range(groups):
                acc[g] = acc[g] + cv * wrow[:, g * LANES:(g + 1) * LANES]
        return tuple(acc)

    acc0 = tuple(jnp.zeros((bsz, LANES), F32) for _ in range(groups))
    acc = lax.fori_loop(0, d_model // SUBLANES, body, acc0)
    for g in range(groups):
        o_ref[0, :, g * LANES:(g + 1) * LANES] = acc[g] + b_ref[0, :, g * LANES:(g + 1) * LANES]


def _ada_modulation(c, ada_w, ada_b):
    depth, d_model, n_out = ada_w.shape
    bsz = c.shape[0]
    tn = _tile(n_out, 1024)
    cb = jnp.broadcast_to(c.T[:, :, None], (d_model, bsz, LANES))
    return pl.pallas_call(
        _ada_kernel,
        grid=(depth, n_out // tn),
        in_specs=[
            pl.BlockSpec((d_model, bsz, LANES), lambda l, j: (0, 0, 0), pipeline_mode=pl.Buffered(1)),
            pl.BlockSpec((1, d_model, tn), lambda l, j: (l, 0, j)),
            pl.BlockSpec((1, 1, tn), lambda l, j: (l, 0, j)),
        ],
        out_specs=pl.BlockSpec((1, bsz, tn), lambda l, j: (l, 0, j)),
        out_shape=jax.ShapeDtypeStruct((depth, bsz, n_out), F32),
        compiler_params=_params(("arbitrary", "arbitrary")),
        name="ada_modulation",
    )(cb, ada_w, ada_b.reshape(depth, 1, n_out))


def _rope_table_kernel(pos_ref, freq_ref, c_ref, s1_ref, s2_ref):
    ang = pos_ref[...].astype(F32) * freq_ref[...]
    lane = lax.broadcasted_iota(jnp.int32, ang.shape, 1) % (LANES // 2)
    half = ROPE_DIM // 2
    cosv = jnp.cos(ang)
    sinv = jnp.sin(ang)
    c_ref[...] = jnp.where(lane < ROPE_DIM, cosv, 1.0)
    s1_ref[...] = jnp.where((lane >= half) & (lane < ROPE_DIM), sinv, 0.0)
    s2_ref[...] = jnp.where(lane < half, -sinv, 0.0)


def _rope_tables(positions):
    n_tok = positions.size
    half = ROPE_DIM // 2
    inv_freq = ROPE_THETA ** (-jnp.arange(0, ROPE_DIM, 2, dtype=F32) / ROPE_DIM)
    sub = jnp.concatenate([inv_freq, inv_freq, jnp.zeros((LANES // 2 - 2 * half,), F32)])
    freq_row = jnp.concatenate([sub, sub]).reshape(1, LANES)
    tm = _tile(n_tok, 2048)
    tab = jax.ShapeDtypeStruct((n_tok, LANES), F32)
    return pl.pallas_call(
        _rope_table_kernel,
        grid=(n_tok // tm,),
        in_specs=[pl.BlockSpec((tm, 1), lambda i: (i, 0)), pl.BlockSpec((1, LANES), lambda i: (0, 0))],
        out_specs=[pl.BlockSpec((tm, LANES), lambda i: (i, 0))] * 3,
        out_shape=[tab, tab, tab],
        compiler_params=_params(("arbitrary",)),
        name="rope_tables",
    )(positions.reshape(n_tok, 1), freq_row)


def _modulated_norm(x, g, sc, sh):
    ms = jnp.mean(x * x, axis=-1, keepdims=True)
    return (x * lax.rsqrt(ms + NORM_EPS)) * g * (1.0 + sc) + sh


def _mix_in_kernel(has_prev, rope_lo, rope_mid, rope_hi, q_scale, *refs):
    if has_prev:
        (x_ref, yt_ref, g2_ref, g_ref, sc_ref, sh_ref, w_ref, rc_ref, rs1_ref, rs2_ref,
         z_ref, xo_ref, h_scr) = refs
    else:
        (x_ref, g_ref, sc_ref, sh_ref, w_ref, rc_ref, rs1_ref, rs2_ref, z_ref, h_scr) = refs
    j = pl.program_id(1)

    @pl.when(j == 0)
    def _():
        x = x_ref[...]
        if has_prev:
            x = x + g2_ref[0] * yt_ref[...].T
            xo_ref[...] = x
        h_scr[...] = _modulated_norm(x, g_ref[...], sc_ref[0], sh_ref[0]).astype(BF16)

    z = jnp.dot(h_scr[...], w_ref[...], preferred_element_type=F32)
    is_rope = (j >= rope_lo) & (j < rope_hi)

    @pl.when(jnp.logical_not(is_rope))
    def _():
        z_ref[...] = z.astype(BF16)

    @pl.when(is_rope)
    def _():
        reps = z.shape[1] // LANES
        scale = jnp.where(j < rope_mid, q_scale, 1.0).astype(F32)
        zr_prev = pltpu.roll(z, ROPE_DIM // 2, axis=1)
        zr_next = pltpu.roll(z, z.shape[1] - ROPE_DIM // 2, axis=1)
        for r in range(reps):
            sl = slice(r * LANES, (r + 1) * LANES)
            out = z[:, sl] * rc_ref[...] + zr_prev[:, sl] * rs1_ref[...] + zr_next[:, sl] * rs2_ref[...]
            z_ref[:, sl] = (out * scale).astype(BF16)


def _mix_in(x, prev, norm_g, sc, sh, w_bf16, rope, seq, q_cols, k_cols, v_cols, q_scale):
    n_tok, d_model = x.shape
    n_out = w_bf16.shape[1]
    tm = _tile(seq, 512)
    tn = _tile(q_cols, 1024)
    assert q_cols % tn == 0 and k_cols % tn == 0 and v_cols % tn == 0
    tiles_per_seq = seq // tm
    has_prev = prev is not None
    row = lambda i, j: (i, 0)
    per_batch = lambda i, j: (i // tiles_per_seq, 0, 0)
    fixed = lambda i, j: (0, 0)
    in_specs = [pl.BlockSpec((tm, d_model), row)]
    args = [x]
    if has_prev:
        yt, g2 = prev
        in_specs += [pl.BlockSpec((d_model, tm), lambda i, j: (0, i)), pl.BlockSpec((1, 1, d_model), per_batch)]
        args += [yt, g2]
    in_specs += [
        pl.BlockSpec((1, d_model), fixed),
        pl.BlockSpec((1, 1, d_model), per_batch),
        pl.BlockSpec((1, 1, d_model), per_batch),
        pl.BlockSpec((d_model, tn), lambda i, j: (0, j)),
        pl.BlockSpec((tm, LANES), row),
        pl.BlockSpec((tm, LANES), row),
        pl.BlockSpec((tm, LANES), row),
    ]
    args += [norm_g.reshape(1, d_model), sc, sh, w_bf16, *rope]
    out_specs = [pl.BlockSpec((tm, tn), lambda i, j: (i, j))]
    out_shape = [jax.ShapeDtypeStruct((n_tok, n_out), BF16)]
    if has_prev:
        out_specs.append(pl.BlockSpec((tm, d_model), row))
        out_shape.append(jax.ShapeDtypeStruct((n_tok, d_model), F32))
    kern = functools.partial(_mix_in_kernel, has_prev, q_cols // tn, k_cols // tn, v_cols // tn, q_scale)
    outs = pl.pallas_call(
        kern,
        grid=(n_tok // tm, n_out // tn),
        in_specs=in_specs,
        out_specs=out_specs,
        out_shape=out_shape,
        scratch_shapes=[pltpu.VMEM((tm, d_model), BF16)],
        compiler_params=_params(("arbitrary", "arbitrary")),
        name="mix_in",
    )(*args)
    return (outs[0], outs[1]) if has_prev else (outs[0], x)


def _attention_kernel(lam_init, q_ref, k_ref, v_ref, lam_ref, g_ref, o_ref):
    tq = q_ref.shape[0]
    q = q_ref[...]
    lane = lax.broadcasted_iota(jnp.int32, q.shape, 1)
    zero = jnp.zeros_like(q)
    qq = jnp.concatenate([jnp.where(lane < LANES // 2, q, zero), jnp.where(lane >= LANES // 2, q, zero)], axis=0)
    s = lax.dot_general(qq, k_ref[...], (((1,), (1,)), ((), ())), preferred_element_type=F32)
    m = jnp.max(s, axis=1, keepdims=True)
    p = jnp.exp(s - m)
    inv_l = 1.0 / jnp.sum(p, axis=1, keepdims=True)
    lv = lam_ref[...]
    lam = (jnp.exp(jnp.sum(lv[0:1] * lv[1:2], axis=1, keepdims=True))
           - jnp.exp(jnp.sum(lv[2:3] * lv[3:4], axis=1, keepdims=True)) + lam_init)
    a = (p[:tq] * inv_l[:tq] - p[tq:] * (lam * inv_l[tq:])).astype(BF16)
    o = jnp.dot(a, v_ref[...], preferred_element_type=F32)
    ms = jnp.mean(o * o, axis=-1, keepdims=True)
    o_ref[...] = ((o * lax.rsqrt(ms + NORM_EPS)) * g_ref[...] * (1.0 - lam_init)).astype(BF16)


def _attention(z, lam_vecs, subln_g, bsz, seq, q_cols, k_cols, v_cols, lam_init):
    n_tok = z.shape[0]
    hd = LANES
    tq = _tile(seq, 256)
    nq = seq // tq
    qb, kb, vb = q_cols // hd, k_cols // hd, v_cols // hd
    return pl.pallas_call(
        functools.partial(_attention_kernel, lam_init),
        grid=(bsz, ATTN_HEADS, nq),
        in_specs=[
            pl.BlockSpec((tq, hd), lambda b, h, i: (b * nq + i, qb + h)),
            pl.BlockSpec((seq, hd), lambda b, h, i: (b, kb + h)),
            pl.BlockSpec((seq, hd), lambda b, h, i: (b, vb + h)),
            pl.BlockSpec(lam_vecs.shape, lambda b, h, i: (0, 0)),
            pl.BlockSpec((1, hd), lambda b, h, i: (0, 0)),
        ],
        out_specs=pl.BlockSpec((tq, hd), lambda b, h, i: (b * nq + i, h)),
        out_shape=jax.ShapeDtypeStruct((n_tok, ATTN_HEADS * hd), BF16),
        compiler_params=_params(("arbitrary", "arbitrary", "arbitrary")),
        name="diff_attention",
    )(z, z, z, lam_vecs, subln_g.reshape(1, hd))


def _mix_out_kernel(tiles_per_seq, gb_ref, gc_ref, hv_ref, gcp_ref, hvp_ref, gcn_ref, hvn_ref, at_ref,
                    cw_ref, w_ref, x_ref, g1_ref, o_ref):
    i = pl.program_id(0)
    tm = gc_ref.shape[0]
    cw = gc_ref.shape[1]
    zc = gc_ref[...].astype(F32) * hv_ref[...].astype(F32)
    first = (i % tiles_per_seq) == 0
    last = (i % tiles_per_seq) == tiles_per_seq - 1
    prev_row = gcp_ref[SUBLANES - 1:SUBLANES, :].astype(F32) * hvp_ref[SUBLANES - 1:SUBLANES, :].astype(F32)
    next_row = gcn_ref[0:1, :].astype(F32) * hvn_ref[0:1, :].astype(F32)
    prev_row = jnp.where(first, 0.0, prev_row)
    next_row = jnp.where(last, 0.0, next_row)
    rows = lax.broadcasted_iota(jnp.int32, zc.shape, 0)
    z_prev = jnp.where(rows == 0, prev_row, pltpu.roll(zc, 1, axis=0))
    z_next = jnp.where(rows == tm - 1, next_row, pltpu.roll(zc, tm - 1, axis=0))
    y = cw_ref[0:1, :] * z_prev + cw_ref[1:2, :] * zc + cw_ref[2:3, :] * z_next
    conv = (gb_ref[...].astype(F32) * y).astype(BF16)
    acc = jnp.dot(conv, w_ref[:cw, :], preferred_element_type=F32)
    acc = acc + jnp.dot(at_ref[...], w_ref[cw:, :], preferred_element_type=F32)
    o_ref[...] = x_ref[...] + g1_ref[0] * acc


def _mix_out(z, attn, conv_w, w_bf16, x, g1, seq, conv_cols):
    n_tok, d_model = x.shape
    cw = conv_cols
    aw = attn.shape[1]
    tm = _tile(seq, 512)
    tiles_per_seq = seq // tm
    hb = tm // SUBLANES
    last_hb = n_tok // SUBLANES - 1
    row = lambda i: (i, 0)
    prev_blk = lambda i: (jnp.maximum(i * hb - 1, 0), 1)
    prev_blk2 = lambda i: (jnp.maximum(i * hb - 1, 0), 2)
    next_blk = lambda i: (jnp.minimum((i + 1) * hb, last_hb), 1)
    next_blk2 = lambda i: (jnp.minimum((i + 1) * hb, last_hb), 2)
    return pl.pallas_call(
        functools.partial(_mix_out_kernel, tiles_per_seq),
        grid=(n_tok // tm,),
        in_specs=[
            pl.BlockSpec((tm, cw), lambda i: (i, 0)),
            pl.BlockSpec((tm, cw), lambda i: (i, 1)),
            pl.BlockSpec((tm, cw), lambda i: (i, 2)),
            pl.BlockSpec((SUBLANES, cw), prev_blk),
            pl.BlockSpec((SUBLANES, cw), prev_blk2),
            pl.BlockSpec((SUBLANES, cw), next_blk),
            pl.BlockSpec((SUBLANES, cw), next_blk2),
            pl.BlockSpec((tm, aw), row),
            pl.BlockSpec((CONV_K, cw), lambda i: (0, 0)),
            pl.BlockSpec((cw + aw, d_model), lambda i: (0, 0), pipeline_mode=pl.Buffered(1)),
            pl.BlockSpec((tm, d_model), row),
            pl.BlockSpec((1, 1, d_model), lambda i: (i // tiles_per_seq, 0, 0)),
        ],
        out_specs=pl.BlockSpec((tm, d_model), row),
        out_shape=jax.ShapeDtypeStruct((n_tok, d_model), F32),
        compiler_params=_params(("arbitrary",)),
        name="mix_out",
    )(z, z, z, z, z, z, z, attn, conv_w, w_bf16, x, g1)


def _split_bf16(a):
    hi = a.astype(BF16)
    lo = (a - hi.astype(F32)).astype(BF16)
    return hi, lo


def _dot3(a_hi, a_lo, b_hi, b_lo):
    d = functools.partial(jnp.dot, preferred_element_type=F32)
    return d(a_hi, b_hi) + (d(a_hi, b_lo) + d(a_lo, b_hi))


def _top_values(s, count):
    vals = []
    work = s
    for k in range(count):
        m = jnp.max(work, axis=0, keepdims=True)
        vals.append(m)
        if k + 1 < count:
            work = jnp.where(work == m, -jnp.inf, work)
    return vals


def _stack_rows(rows):
    n = len(rows)
    idx = lax.broadcasted_iota(jnp.int32, (n, rows[0].shape[1]), 0)
    out = jnp.broadcast_to(rows[0], idx.shape)
    for k in range(1, n):
        out = jnp.where(idx == k, rows[k], out)
    return out


def _peer_score_kernel(x_ref, g_ref, sc_ref, sh_ref, wqh_ref, wql_ref, kh_ref, kl_ref,
                       ht_ref, t1_ref, s2_ref, tau_ref):
    nk = kh_ref.shape[2]
    h = _modulated_norm(x_ref[...], g_ref[...], sc_ref[0], sh_ref[0])
    ht = h.T
    ht_hi, ht_lo = _split_bf16(ht)
    ht_ref[...] = pltpu.bitcast(ht_hi, jnp.uint32)
    qt = _dot3(wqh_ref[...], wql_ref[...], ht_hi, ht_lo)
    half = kh_ref.shape[3]
    k = PEER_TOPK
    for hd in range(PEER_HEADS):
        st = []
        for c in range(2):
            r0 = (hd * 2 + c) * half
            q_hi, q_lo = _split_bf16(qt[r0:r0 + half, :])
            st.append(_dot3(kh_ref[hd, c], kl_ref[hd, c], q_hi, q_lo))
        a = _top_values(st[0], k)
        b = _top_values(st[1], k)
        b_all = _stack_rows(b)
        a_tail = _stack_rows(a[k // 2:])
        cand = [a[0] + b_all]
        cand += [a[p] + b_all[:k // 2] for p in range(1, k // 2)]
        cand.append(a_tail + b[0])
        cand = jnp.concatenate(cand, axis=0)
        top = _top_values(cand, k + 1)
        tau = 0.5 * (top[k - 1] + top[k])
        z = jnp.sum(jnp.where(cand > tau, jnp.exp(cand - top[0]), 0.0), axis=0, keepdims=True)
        off = top[0] + jnp.log(z)
        t1_ref[hd * nk:(hd + 1) * nk, :] = (st[0] - off) * LOG2E
        s2_ref[hd * nk:(hd + 1) * nk, :] = st[1] * LOG2E
        tau_ref[hd:hd + 1, :] = (tau - off) * LOG2E


def _peer_score(x, norm_g, sc, sh, wqt_hi, wqt_lo, keys_hi, keys_lo, seq):
    n_tok, d_model = x.shape
    nk = keys_hi.shape[2]
    tm = _tile(seq, 256)
    tiles_per_seq = seq // tm
    per_batch = lambda i: (i // tiles_per_seq, 0, 0)
    once = pl.Buffered(1)
    col = lambda i: (0, i)
    return pl.pallas_call(
        _peer_score_kernel,
        grid=(n_tok // tm,),
        in_specs=[
            pl.BlockSpec((tm, d_model), lambda i: (i, 0)),
            pl.BlockSpec((1, d_model), lambda i: (0, 0)),
            pl.BlockSpec((1, 1, d_model), per_batch),
            pl.BlockSpec((1, 1, d_model), per_batch),
            pl.BlockSpec(wqt_hi.shape, lambda i: (0, 0), pipeline_mode=once),
            pl.BlockSpec(wqt_lo.shape, lambda i: (0, 0), pipeline_mode=once),
            pl.BlockSpec(keys_hi.shape, lambda i: (0, 0, 0, 0), pipeline_mode=once),
            pl.BlockSpec(keys_lo.shape, lambda i: (0, 0, 0, 0), pipeline_mode=once),
        ],
        out_specs=[
            pl.BlockSpec((d_model // 2, tm), col),
            pl.BlockSpec((PEER_HEADS * nk, tm), col),
            pl.BlockSpec((PEER_HEADS * nk, tm), col),
            pl.BlockSpec((PEER_HEADS, tm), col),
        ],
        out_shape=[
            jax.ShapeDtypeStruct((d_model // 2, n_tok), jnp.uint32),
            jax.ShapeDtypeStruct((PEER_HEADS * nk, n_tok), F32),
            jax.ShapeDtypeStruct((PEER_HEADS * nk, n_tok), F32),
            jax.ShapeDtypeStruct((PEER_HEADS, n_tok), F32),
        ],
        compiler_params=_params(("arbitrary",)),
        name="peer_score",
    )(x, norm_g.reshape(1, d_model), sc, sh, wqt_hi, wqt_lo, keys_hi, keys_lo)


GATE_ROWS = 32
PIPE_COLS = 256
LOG2E = 1.0 / math.log(2.0)


def _peer_main_kernel(nk, u_ref, vt_ref, ht_ref, t1_ref, s2_ref, tau_ref, o_ref, a_scr, ag_scr):
    e = pl.program_id(1)
    eb, tm = a_scr.shape
    d_model = o_ref.shape[0]
    nb = eb // nk
    groups = GATE_ROWS // SUBLANES
    n_chunks = tm // PIPE_COLS
    pieces = d_model // MXU_DIM
    b_rows = eb // pieces
    c_rows = d_model // pieces
    assert nk % b_rows == 0 and b_rows % GATE_ROWS == 0

    @pl.when(e == 0)
    def _():
        o_ref[...] = jnp.zeros_like(o_ref)

    def wide(cc):
        return slice(cc * PIPE_COLS, (cc + 1) * PIPE_COLS)

    def stage_a(cc, p, acc):
        ks = slice(p * MXU_DIM, (p + 1) * MXU_DIM)
        part = jnp.dot(pltpu.bitcast(u_ref[:, ks], BF16), pltpu.bitcast(ht_ref[p * MXU_DIM // 2:(p + 1) * MXU_DIM // 2, wide(cc)], BF16),
                       preferred_element_type=F32)
        acc = part if acc is None else acc + part
        if p == pieces - 1:
            a_scr[:, wide(cc)] = acc
        return acc

    def stage_b(cc, p):
        ii, r_base = divmod(p * b_rows, nk)
        for sub in range(PIPE_COLS // LANES):
            cols = slice(cc * PIPE_COLS + sub * LANES, cc * PIPE_COLS + (sub + 1) * LANES)
            for r0 in range(r_base, r_base + b_rows, GATE_ROWS):
                gate = [None] * groups
                for hd in range(PEER_HEADS):
                    tau_b = jnp.broadcast_to(tau_ref[hd:hd + 1, cols], (SUBLANES, LANES))
                    t1_b = jnp.broadcast_to(t1_ref[0, hd * nb + ii:hd * nb + ii + 1, cols], (SUBLANES, LANES))
                    for g in range(groups):
                        row = hd * nk + r0 + g * SUBLANES
                        sm = t1_b + s2_ref[row:row + SUBLANES, cols]
                        w = jnp.where(sm > tau_b, jnp.exp2(sm), 0.0)
                        gate[g] = w if gate[g] is None else gate[g] + w
                for g in range(groups):
                    rows = slice(ii * nk + r0 + g * SUBLANES, ii * nk + r0 + (g + 1) * SUBLANES)
                    act = a_scr[rows, cols]
                    gel = 0.5 * act * (1.0 + lax.erf(act * (1.0 / math.sqrt(2.0))))
                    ag_scr[rows, cols] = (gel * gate[g]).astype(BF16)

    def stage_c(cc, p):
        rows = slice(p * c_rows, (p + 1) * c_rows)
        o_ref[rows, wide(cc)] += jnp.dot(vt_ref[rows, :], ag_scr[:, wide(cc)], preferred_element_type=F32)

    for slot in range(n_chunks + 2):
        acc = None
        for p in range(pieces):
            if slot < n_chunks:
                acc = stage_a(slot, p, acc)
            if 0 <= slot - 1 < n_chunks:
                stage_b(slot - 1, p)
            if 0 <= slot - 2 < n_chunks:
                stage_c(slot - 2, p)


def _pack_row_pairs(a_bf16):
    m2, n = a_bf16.shape
    return lax.bitcast_convert_type(jnp.swapaxes(a_bf16.reshape(m2 // 2, 2, n), 1, 2), jnp.uint32)


def _peer_main(u_pk, vt_bf16, ht, t1, s2, tau, nk):
    n_exp, d_model = 2 * u_pk.shape[0], u_pk.shape[1]
    n_tok = ht.shape[1]
    tm = _tile(n_tok, 1024)
    eb = _tile(n_exp, 512)
    n_blocks = n_exp // eb
    once = pl.Buffered(1)
    col = lambda i, e: (0, i)
    nb = eb // nk
    t1 = t1.reshape(PEER_HEADS, n_blocks, nb, n_tok).transpose(1, 0, 2, 3).reshape(n_blocks, PEER_HEADS * nb, n_tok)
    return pl.pallas_call(
        functools.partial(_peer_main_kernel, nk),
        grid=(n_tok // tm, n_blocks),
        in_specs=[
            pl.BlockSpec((eb // 2, d_model), lambda i, e: (e, 0)),
            pl.BlockSpec((d_model, eb), lambda i, e: (0, e)),
            pl.BlockSpec((d_model // 2, tm), col, pipeline_mode=once),
            pl.BlockSpec((1, PEER_HEADS * nb, tm), lambda i, e: (e, 0, i)),
            pl.BlockSpec((s2.shape[0], tm), col, pipeline_mode=once),
            pl.BlockSpec((tau.shape[0], tm), col, pipeline_mode=once),
        ],
        out_specs=pl.BlockSpec((d_model, tm), col),
        out_shape=jax.ShapeDtypeStruct((d_model, n_tok), F32),
        scratch_shapes=[pltpu.VMEM((eb, tm), F32), pltpu.VMEM((eb, tm), BF16)],
        compiler_params=_params(("arbitrary", "arbitrary")),
        name="peer_main",
    )(u_pk, vt_bf16, ht, t1, s2, tau)


def _final_kernel(x_ref, yt_ref, g2_ref, g_ref, o_ref):
    x = x_ref[...] + g2_ref[0] * yt_ref[...].T
    ms = jnp.mean(x * x, axis=-1, keepdims=True)
    o_ref[...] = (x * lax.rsqrt(ms + NORM_EPS)) * g_ref[...]


def _final(x, yt, g2, final_g, seq):
    n_tok, d_model = x.shape
    tm = _tile(seq, 512)
    tiles_per_seq = seq // tm
    return pl.pallas_call(
        _final_kernel,
        grid=(n_tok // tm,),
        in_specs=[
            pl.BlockSpec((tm, d_model), lambda i: (i, 0)),
            pl.BlockSpec((d_model, tm), lambda i: (0, i)),
            pl.BlockSpec((1, 1, d_model), lambda i: (i // tiles_per_seq, 0, 0)),
            pl.BlockSpec((1, d_model), lambda i: (0, 0)),
        ],
        out_specs=pl.BlockSpec((tm, d_model), lambda i: (i, 0)),
        out_shape=jax.ShapeDtypeStruct((n_tok, d_model), F32),
        compiler_params=_params(("arbitrary",)),
        name="final_norm",
    )(x, yt, g2, final_g.reshape(1, d_model))


def kernel(x, c, positions, ada_w, ada_b, norm1_g, w_in, conv_w, lambda_q1, lambda_k1, lambda_q2, lambda_k2,
           subln_g, w_out, norm2_g, peer_wq, peer_keys, peer_u, peer_v, final_g):
    bsz, seq, d_model = x.shape
    depth = ada_w.shape[0]
    n_tok = bsz * seq
    conv_cols = conv_w.shape[2]
    attn_cols = w_out.shape[1] - conv_cols
    q_cols = 3 * conv_cols
    k_cols = q_cols + attn_cols
    v_cols = k_cols + attn_cols
    nk = peer_keys.shape[3]
    sub_dim = attn_cols // ATTN_HEADS // 2
    q_scale = sub_dim ** -0.5

    rope = _rope_tables(positions)
    mod = _ada_modulation(c, ada_w, ada_b)
    xf = x.reshape(n_tok, d_model)
    prev = None
    for l in range(depth):
        lam_init = 0.8 - 0.6 * math.exp(-0.3 * l)
        sh1, sc1, g1, sh2, sc2, g2 = [m.reshape(bsz, 1, d_model) for m in jnp.split(mod[l], 6, axis=-1)]
        z, xf = _mix_in(xf, prev, norm1_g[l], sc1, sh1, w_in[l].astype(BF16), rope, seq,
                        q_cols, k_cols, v_cols, q_scale)
        lam_vecs = jnp.stack([lambda_q1[l], lambda_k1[l], lambda_q2[l], lambda_k2[l]])
        attn = _attention(z, lam_vecs, subln_g[l], bsz, seq, q_cols, k_cols, v_cols, lam_init)
        xf = _mix_out(z, attn, conv_w[l], w_out[l].astype(BF16), xf, g1, seq, conv_cols)
        wqt_hi, wqt_lo = _split_bf16(peer_wq[l].T)
        keys_hi, keys_lo = _split_bf16(peer_keys[l])
        ht, t1, s2, tau = _peer_score(xf, norm2_g[l], sc2, sh2, wqt_hi, wqt_lo, keys_hi, keys_lo, seq)
        yt = _peer_main(_pack_row_pairs(peer_u[l].astype(BF16)), peer_v[l].T.astype(BF16), ht, t1, s2, tau, nk)
        prev = (yt, g2)
    out = _final(xf, prev[0], prev[1], final_g, seq)
    return out.reshape(bsz, seq, d_model)
```

```python
import functools
import math

import jax
import jax.numpy as jnp
from jax import lax
from jax.experimental import pallas as pl
from jax.experimental.pallas import tpu as pltpu

F32 = jnp.float32
BF16 = jnp.bfloat16

NORM_EPS = 1e-6
ROPE_THETA = 500000.0
ATTN_HEADS = 8
ROPE_DIM = 16
PEER_HEADS = 8
PEER_TOPK = 16
CONV_K = 3

MXU_DIM = 256
LANES = 128
SUBLANES = 8
VMEM_LIMIT = 56 * 1024 * 1024


def _tile(n, pref):
    return pref if n % pref == 0 else n


def _params(sem):
    return pltpu.CompilerParams(dimension_semantics=sem, vmem_limit_bytes=VMEM_LIMIT)


def _ada_kernel(cb_ref, w_ref, b_ref, o_ref):
    d_model = w_ref.shape[1]
    tn = w_ref.shape[2]
    bsz = cb_ref.shape[1]
    groups = tn // LANES

    def body(d8, acc):
        acc = list(acc)
        for r in range(SUBLANES):
            d = d8 * SUBLANES + r
            cv = cb_ref[d]
            cv = cv / (1.0 + jnp.exp(-cv))
            wrow = w_ref[0, pl.ds(d, 1), :]
            for g in range(groups):
                acc[g] = acc[g] + cv * wrow[:, g * LANES:(g + 1) * LANES]
        return tuple(acc)

    acc0 = tuple(jnp.zeros((bsz, LANES), F32) for _ in range(groups))
    acc = lax.fori_loop(0, d_model // SUBLANES, body, acc0)
    for g in range(groups):
        o_ref[0, :, g * LANES:(g + 1) * LANES] = acc[g] + b_ref[0, :, g * LANES:(g + 1) * LANES]


def _ada_modulation(c, ada_w, ada_b):
    depth, d_model, n_out = ada_w.shape
    bsz = c.shape[0]
    tn = _tile(n_out, 1024)
    cb = jnp.broadcast_to(c.T[:, :, None], (d_model, bsz, LANES))
    return pl.pallas_call(
        _ada_kernel,
        grid=(depth, n_out // tn),
        in_specs=[
            pl.BlockSpec((d_model, bsz, LANES), lambda l, j: (0, 0, 0), pipeline_mode=pl.Buffered(1)),
            pl.BlockSpec((1, d_model, tn), lambda l, j: (l, 0, j)),
            pl.BlockSpec((1, 1, tn), lambda l, j: (l, 0, j)),
        ],
        out_specs=pl.BlockSpec((1, bsz, tn), lambda l, j: (l, 0, j)),
        out_shape=jax.ShapeDtypeStruct((depth, bsz, n_out), F32),
        compiler_params=_params(("arbitrary", "arbitrary")),
        name="ada_modulation",
    )(cb, ada_w, ada_b.reshape(depth, 1, n_out))


def _rope_table_kernel(pos_ref, freq_ref, c_ref, s1_ref, s2_ref):
    ang = pos_ref[...].astype(F32) * freq_ref[...]
    lane = lax.broadcasted_iota(jnp.int32, ang.shape, 1) % (LANES // 2)
    half = ROPE_DIM // 2
    cosv = jnp.cos(ang)
    sinv = jnp.sin(ang)
    c_ref[...] = jnp.where(lane < ROPE_DIM, cosv, 1.0)
    s1_ref[...] = jnp.where((lane >= half) & (lane < ROPE_DIM), sinv, 0.0)
    s2_ref[...] = jnp.where(lane < half, -sinv, 0.0)


def _rope_tables(positions):
    n_tok = positions.size
    half = ROPE_DIM // 2
    inv_freq = ROPE_THETA ** (-jnp.arange(0, ROPE_DIM, 2, dtype=F32) / ROPE_DIM)
    sub = jnp.concatenate([inv_freq, inv_freq, jnp.zeros((LANES // 2 - 2 * half,), F32)])
    freq_row = jnp.concatenate([sub, sub]).reshape(1, LANES)
    tm = _tile(n_tok, 2048)
    tab = jax.ShapeDtypeStruct((n_tok, LANES), F32)
    return pl.pallas_call(
        _rope_table_kernel,
        grid=(n_tok // tm,),
        in_specs=[pl.BlockSpec((tm, 1), lambda i: (i, 0)), pl.BlockSpec((1, LANES), lambda i: (0, 0))],
        out_specs=[pl.BlockSpec((tm, LANES), lambda i: (i, 0))] * 3,
        out_shape=[tab, tab, tab],
        compiler_params=_params(("arbitrary",)),
        name="rope_tables",
    )(positions.reshape(n_tok, 1), freq_row)


def _modulated_norm(x, g, sc, sh):
    ms = jnp.mean(x * x, axis=-1, keepdims=True)
    return (x * lax.rsqrt(ms + NORM_EPS)) * g * (1.0 + sc) + sh


def _mix_in_kernel(has_prev, rope_lo, rope_mid, rope_hi, q_scale, *refs):
    if has_prev:
        (x_ref, yt_ref, g2_ref, g_ref, sc_ref, sh_ref, w_ref, rc_ref, rs1_ref, rs2_ref,
         z_ref, xo_ref, h_scr) = refs
    else:
        (x_ref, g_ref, sc_ref, sh_ref, w_ref, rc_ref, rs1_ref, rs2_ref, z_ref, h_scr) = refs
    j = pl.program_id(1)

    @pl.when(j == 0)
    def _():
        x = x_ref[...]
        if has_prev:
            x = x + g2_ref[0] * yt_ref[...].T
            xo_ref[...] = x
        h_scr[...] = _modulated_norm(x, g_ref[...], sc_ref[0], sh_ref[0]).astype(BF16)

    z = jnp.dot(h_scr[...], pltpu.bitcast(w_ref[...], BF16), preferred_element_type=F32)
    is_rope = (j >= rope_lo) & (j < rope_hi)

    @pl.when(jnp.logical_not(is_rope))
    def _():
        z_ref[...] = z.astype(BF16)

    @pl.when(is_rope)
    def _():
        reps = z.shape[1] // LANES
        scale = jnp.where(j < rope_mid, q_scale, 1.0).astype(F32)
        zr_prev = pltpu.roll(z, ROPE_DIM // 2, axis=1)
        zr_next = pltpu.roll(z, z.shape[1] - ROPE_DIM // 2, axis=1)
        for r in range(reps):
            sl = slice(r * LANES, (r + 1) * LANES)
            out = z[:, sl] * rc_ref[...] + zr_prev[:, sl] * rs1_ref[...] + zr_next[:, sl] * rs2_ref[...]
            z_ref[:, sl] = (out * scale).astype(BF16)


def _mix_in(x, prev, norm_g, sc, sh, w_pk, rope, seq, q_cols, k_cols, v_cols, q_scale):
    n_tok, d_model = x.shape
    n_out = w_pk.shape[1]
    tm = _tile(seq, 512)
    tn = _tile(q_cols, 1024)
    assert q_cols % tn == 0 and k_cols % tn == 0 and v_cols % tn == 0
    tiles_per_seq = seq // tm
    has_prev = prev is not None
    row = lambda i, j: (i, 0)
    per_batch = lambda i, j: (i // tiles_per_seq, 0, 0)
    fixed = lambda i, j: (0, 0)
    in_specs = [pl.BlockSpec((tm, d_model), row)]
    args = [x]
    if has_prev:
        yt, g2 = prev
        in_specs += [pl.BlockSpec((d_model, tm), lambda i, j: (0, i)), pl.BlockSpec((1, 1, d_model), per_batch)]
        args += [yt, g2]
    in_specs += [
        pl.BlockSpec((1, d_model), fixed),
        pl.BlockSpec((1, 1, d_model), per_batch),
        pl.BlockSpec((1, 1, d_model), per_batch),
        pl.BlockSpec((d_model // 2, tn), lambda i, j: (0, j)),
        pl.BlockSpec((tm, LANES), row),
        pl.BlockSpec((tm, LANES), row),
        pl.BlockSpec((tm, LANES), row),
    ]
    args += [norm_g.reshape(1, d_model), sc, sh, w_pk, *rope]
    out_specs = [pl.BlockSpec((tm, tn), lambda i, j: (i, j))]
    out_shape = [jax.ShapeDtypeStruct((n_tok, n_out), BF16)]
    if has_prev:
        out_specs.append(pl.BlockSpec((tm, d_model), row))
        out_shape.append(jax.ShapeDtypeStruct((n_tok, d_model), F32))
    kern = functools.partial(_mix_in_kernel, has_prev, q_cols // tn, k_cols // tn, v_cols // tn, q_scale)
    outs = pl.pallas_call(
        kern,
        grid=(n_tok // tm, n_out // tn),
        in_specs=in_specs,
        out_specs=out_specs,
        out_shape=out_shape,
        scratch_shapes=[pltpu.VMEM((tm, d_model), BF16)],
        compiler_params=_params(("arbitrary", "arbitrary")),
        name="mix_in",
    )(*args)
    return (outs[0], outs[1]) if has_prev else (outs[0], x)


def _attention_kernel(lam_init, q_ref, k_ref, v_ref, lam_ref, g_ref, o_ref):
    tq = q_ref.shape[0]
    q = q_ref[...]
    lane = lax.broadcasted_iota(jnp.int32, q.shape, 1)
    zero = jnp.zeros_like(q)
    qq = jnp.concatenate([jnp.where(lane < LANES // 2, q, zero), jnp.where(lane >= LANES // 2, q, zero)], axis=0)
    s = lax.dot_general(qq, k_ref[...], (((1,), (1,)), ((), ())), preferred_element_type=F32)
    m = jnp.max(s, axis=1, keepdims=True)
    p = jnp.exp2(s - m)
    l = jnp.sum(p, axis=1, keepdims=True)
    lv = lam_ref[...]
    lam = (jnp.exp(jnp.sum(lv[0:1] * lv[1:2], axis=1, keepdims=True))
           - jnp.exp(jnp.sum(lv[2:3] * lv[3:4], axis=1, keepdims=True)) + lam_init)
    r = lam * l[:tq] / l[tq:]
    a = (p[:tq] - p[tq:] * r).astype(BF16)
    o = jnp.dot(a, v_ref[...], preferred_element_type=F32) * (1.0 / l[:tq])
    ms = jnp.mean(o * o, axis=-1, keepdims=True)
    o_ref[...] = ((o * lax.rsqrt(ms + NORM_EPS)) * g_ref[...] * (1.0 - lam_init)).astype(BF16)


def _attention(z, lam_vecs, subln_g, bsz, seq, q_cols, k_cols, v_cols, lam_init):
    n_tok = z.shape[0]
    hd = LANES
    tq = _tile(seq, 256)
    nq = seq // tq
    qb, kb, vb = q_cols // hd, k_cols // hd, v_cols // hd
    return pl.pallas_call(
        functools.partial(_attention_kernel, lam_init),
        grid=(bsz, ATTN_HEADS, nq),
        in_specs=[
            pl.BlockSpec((tq, hd), lambda b, h, i: (b * nq + i, qb + h)),
            pl.BlockSpec((seq, hd), lambda b, h, i: (b, kb + h)),
            pl.BlockSpec((seq, hd), lambda b, h, i: (b, vb + h)),
            pl.BlockSpec(lam_vecs.shape, lambda b, h, i: (0, 0)),
            pl.BlockSpec((1, hd), lambda b, h, i: (0, 0)),
        ],
        out_specs=pl.BlockSpec((tq, hd), lambda b, h, i: (b * nq + i, h)),
        out_shape=jax.ShapeDtypeStruct((n_tok, ATTN_HEADS * hd), BF16),
        compiler_params=_params(("arbitrary", "arbitrary", "arbitrary")),
        name="diff_attention",
    )(z, z, z, lam_vecs, subln_g.reshape(1, hd))


def _mix_out_kernel(tiles_per_seq, gb_ref, gc_ref, hv_ref, gcp_ref, hvp_ref, gcn_ref, hvn_ref, at_ref,
                    cw_ref, w_ref, x_ref, g1_ref, o_ref):
    i = pl.program_id(0)
    tm = gc_ref.shape[0]
    cw = gc_ref.shape[1]
    zc = gc_ref[...].astype(F32) * hv_ref[...].astype(F32)
    first = (i % tiles_per_seq) == 0
    last = (i % tiles_per_seq) == tiles_per_seq - 1
    prev_row = gcp_ref[SUBLANES - 1:SUBLANES, :].astype(F32) * hvp_ref[SUBLANES - 1:SUBLANES, :].astype(F32)
    next_row = gcn_ref[0:1, :].astype(F32) * hvn_ref[0:1, :].astype(F32)
    prev_row = jnp.where(first, 0.0, prev_row)
    next_row = jnp.where(last, 0.0, next_row)
    rows = lax.broadcasted_iota(jnp.int32, zc.shape, 0)
    z_prev = jnp.where(rows == 0, prev_row, pltpu.roll(zc, 1, axis=0))
    z_next = jnp.where(rows == tm - 1, next_row, pltpu.roll(zc, tm - 1, axis=0))
    y = cw_ref[0:1, :] * z_prev + cw_ref[1:2, :] * zc + cw_ref[2:3, :] * z_next
    conv = (gb_ref[...].astype(F32) * y).astype(BF16)
    acc = jnp.dot(conv, pltpu.bitcast(w_ref[:cw // 2, :], BF16), preferred_element_type=F32)
    acc = acc + jnp.dot(at_ref[...], pltpu.bitcast(w_ref[cw // 2:, :], BF16), preferred_element_type=F32)
    o_ref[...] = x_ref[...] + g1_ref[0] * acc


def _mix_out(z, attn, conv_w, w_pk, x, g1, seq, conv_cols):
    n_tok, d_model = x.shape
    cw = conv_cols
    aw = attn.shape[1]
    tm = _tile(seq, 512)
    tiles_per_seq = seq // tm
    hb = tm // SUBLANES
    last_hb = n_tok // SUBLANES - 1
    row = lambda i: (i, 0)
    prev_blk = lambda i: (jnp.maximum(i * hb - 1, 0), 1)
    prev_blk2 = lambda i: (jnp.maximum(i * hb - 1, 0), 2)
    next_blk = lambda i: (jnp.minimum((i + 1) * hb, last_hb), 1)
    next_blk2 = lambda i: (jnp.minimum((i + 1) * hb, last_hb), 2)
    return pl.pallas_call(
        functools.partial(_mix_out_kernel, tiles_per_seq),
        grid=(n_tok // tm,),
        in_specs=[
            pl.BlockSpec((tm, cw), lambda i: (i, 0)),
            pl.BlockSpec((tm, cw), lambda i: (i, 1)),
            pl.BlockSpec((tm, cw), lambda i: (i, 2)),
            pl.BlockSpec((SUBLANES, cw), prev_blk),
            pl.BlockSpec((SUBLANES, cw), prev_blk2),
            pl.BlockSpec((SUBLANES, cw), next_blk),
            pl.BlockSpec((SUBLANES, cw), next_blk2),
            pl.BlockSpec((tm, aw), row),
            pl.BlockSpec((CONV_K, cw), lambda i: (0, 0)),
            pl.BlockSpec(((cw + aw) // 2, d_model), lambda i: (0, 0), pipeline_mode=pl.Buffered(1)),
            pl.BlockSpec((tm, d_model), row),
            pl.BlockSpec((1, 1, d_model), lambda i: (i // tiles_per_seq, 0, 0)),
        ],
        out_specs=pl.BlockSpec((tm, d_model), row),
        out_shape=jax.ShapeDtypeStruct((n_tok, d_model), F32),
        compiler_params=_params(("arbitrary",)),
        name="mix_out",
    )(z, z, z, z, z, z, z, attn, conv_w, w_pk, x, g1)


def _split_bf16(a):
    hi = a.astype(BF16)
    lo = (a - hi.astype(F32)).astype(BF16)
    return hi, lo


def _dot3(a_hi, a_lo, b_hi, b_lo):
    d = functools.partial(jnp.dot, preferred_element_type=F32)
    return d(a_hi, b_hi) + (d(a_hi, b_lo) + d(a_lo, b_hi))


def _top_values(s, count):
    vals = []
    work = s
    for k in range(count):
        m = jnp.max(work, axis=0, keepdims=True)
        vals.append(m)
        if k + 1 < count:
            work = jnp.where(work == m, -jnp.inf, work)
    return vals


def _stack_rows(rows):
    n = len(rows)
    idx = lax.broadcasted_iota(jnp.int32, (n, rows[0].shape[1]), 0)
    out = jnp.broadcast_to(rows[0], idx.shape)
    for k in range(1, n):
        out = jnp.where(idx == k, rows[k], out)
    return out


def _peer_score_kernel(x_ref, g_ref, sc_ref, sh_ref, wq_ref, kh_ref, kl_ref,
                       ht_ref, t1_ref, s2_ref, tau_ref):
    nk = kh_ref.shape[2]
    h = _modulated_norm(x_ref[...], g_ref[...], sc_ref[0], sh_ref[0])
    ht = h.T
    ht_hi = ht.astype(BF16)
    ht_ref[...] = pltpu.bitcast(ht_hi, jnp.uint32)
    qt = jnp.dot(pltpu.bitcast(wq_ref[...], BF16), ht_hi, preferred_element_type=F32)
    half = kh_ref.shape[3]
    k = PEER_TOPK
    for hd in range(PEER_HEADS):
        st = []
        for c in range(2):
            r0 = (hd * 2 + c) * half
            q_hi, q_lo = _split_bf16(qt[r0:r0 + half, :])
            st.append(_dot3(kh_ref[hd, c], kl_ref[hd, c], q_hi, q_lo))
        a = _top_values(st[0], k)
        b = _top_values(st[1], k)
        b_all = _stack_rows(b)
        a_tail = _stack_rows(a[k // 2:])
        cand = [a[0] + b_all]
        cand += [a[p] + b_all[:k // 2] for p in range(1, k // 2)]
        cand.append(a_tail + b[0])
        cand = jnp.concatenate(cand, axis=0)
        top = _top_values(cand, k + 1)
        tau = 0.5 * (top[k - 1] + top[k])
        z = jnp.sum(jnp.where(cand > tau, jnp.exp(cand - top[0]), 0.0), axis=0, keepdims=True)
        off = top[0] + jnp.log(z)
        t1_ref[hd * nk:(hd + 1) * nk, :] = (st[0] - off) * LOG2E
        s2_ref[hd * nk:(hd + 1) * nk, :] = st[1] * LOG2E
        tau_ref[hd:hd + 1, :] = (tau - off) * LOG2E


def _peer_score(x, norm_g, sc, sh, wqt_pk, keys_hi, keys_lo, seq):
    n_tok, d_model = x.shape
    nk = keys_hi.shape[2]
    tm = _tile(seq, 256)
    tiles_per_seq = seq // tm
    per_batch = lambda i: (i // tiles_per_seq, 0, 0)
    once = pl.Buffered(1)
    col = lambda i: (0, i)
    return pl.pallas_call(
        _peer_score_kernel,
        grid=(n_tok // tm,),
        in_specs=[
            pl.BlockSpec((tm, d_model), lambda i: (i, 0)),
            pl.BlockSpec((1, d_model), lambda i: (0, 0)),
            pl.BlockSpec((1, 1, d_model), per_batch),
            pl.BlockSpec((1, 1, d_model), per_batch),
            pl.BlockSpec(wqt_pk.shape, lambda i: (0, 0), pipeline_mode=once),
            pl.BlockSpec(keys_hi.shape, lambda i: (0, 0, 0, 0), pipeline_mode=once),
            pl.BlockSpec(keys_lo.shape, lambda i: (0, 0, 0, 0), pipeline_mode=once),
        ],
        out_specs=[
            pl.BlockSpec((d_model // 2, tm), col),
            pl.BlockSpec((PEER_HEADS * nk, tm), col),
            pl.BlockSpec((PEER_HEADS * nk, tm), col),
            pl.BlockSpec((PEER_HEADS, tm), col),
        ],
        out_shape=[
            jax.ShapeDtypeStruct((d_model // 2, n_tok), jnp.uint32),
            jax.ShapeDtypeStruct((PEER_HEADS * nk, n_tok), F32),
            jax.ShapeDtypeStruct((PEER_HEADS * nk, n_tok), F32),
            jax.ShapeDtypeStruct((PEER_HEADS, n_tok), F32),
        ],
        compiler_params=_params(("arbitrary",)),
        name="peer_score",
    )(x, norm_g.reshape(1, d_model), sc, sh, wqt_pk, keys_hi, keys_lo)


GATE_ROWS = 32
PIPE_COLS = 256
LOG2E = 1.0 / math.log(2.0)


def _peer_main_kernel(nk, u_ref, vt_ref, ht_ref, t1_ref, s2_ref, tau_ref, o_ref, a_scr, ag_scr):
    e = pl.program_id(1)
    eb, tm = a_scr.shape
    d_model = o_ref.shape[0]
    nb = eb // nk
    groups = GATE_ROWS // SUBLANES
    n_chunks = tm // PIPE_COLS
    pieces = d_model // MXU_DIM
    b_rows = eb // pieces
    c_rows = d_model // pieces
    assert nk % b_rows == 0 and b_rows % GATE_ROWS == 0

    @pl.when(e == 0)
    def _():
        o_ref[...] = jnp.zeros_like(o_ref)

    def wide(cc):
        return slice(cc * PIPE_COLS, (cc + 1) * PIPE_COLS)

    def stage_a(cc, p, acc):
        ks = slice(p * MXU_DIM, (p + 1) * MXU_DIM)
        part = jnp.dot(pltpu.bitcast(u_ref[:, ks], BF16), pltpu.bitcast(ht_ref[p * MXU_DIM // 2:(p + 1) * MXU_DIM // 2, wide(cc)], BF16),
                       preferred_element_type=F32)
        acc = part if acc is None else acc + part
        if p == pieces - 1:
            a_scr[:, wide(cc)] = acc
        return acc

    def stage_b(cc, p):
        ii, r_base = divmod(p * b_rows, nk)
        for sub in range(PIPE_COLS // LANES):
            cols = slice(cc * PIPE_COLS + sub * LANES, cc * PIPE_COLS + (sub + 1) * LANES)
            for r0 in range(r_base, r_base + b_rows, GATE_ROWS):
                gate = [None] * groups
                for hd in range(PEER_HEADS):
                    tau_b = jnp.broadcast_to(tau_ref[hd:hd + 1, cols], (SUBLANES, LANES))
                    t1_b = jnp.broadcast_to(t1_ref[0, hd * nb + ii:hd * nb + ii + 1, cols], (SUBLANES, LANES))
                    for g in range(groups):
                        row = hd * nk + r0 + g * SUBLANES
                        sm = t1_b + s2_ref[row:row + SUBLANES, cols]
                        w = jnp.where(sm > tau_b, jnp.exp2(sm), 0.0)
                        gate[g] = w if gate[g] is None else gate[g] + w
                for g in range(groups):
                    rows = slice(ii * nk + r0 + g * SUBLANES, ii * nk + r0 + (g + 1) * SUBLANES)
                    act = a_scr[rows, cols]
                    gel = 0.5 * act * (1.0 + lax.erf(act * (1.0 / math.sqrt(2.0))))
                    ag_scr[rows, cols] = (gel * gate[g]).astype(BF16)

    def stage_c(cc, p):
        rows = slice(p * c_rows, (p + 1) * c_rows)
        o_ref[rows, wide(cc)] += jnp.dot(vt_ref[rows, :], ag_scr[:, wide(cc)], preferred_element_type=F32)

    for slot in range(n_chunks + 2):
        acc = None
        for p in range(pieces):
            if slot < n_chunks:
                acc = stage_a(slot, p, acc)
            if 0 <= slot - 1 < n_chunks:
                stage_b(slot - 1, p)
            if 0 <= slot - 2 < n_chunks:
                stage_c(slot - 2, p)


def _pack_row_pairs(a_bf16):
    m2, n = a_bf16.shape
    return lax.bitcast_convert_type(jnp.swapaxes(a_bf16.reshape(m2 // 2, 2, n), 1, 2), jnp.uint32)


def _peer_main(u_pk, vt_bf16, ht, t1, s2, tau, nk):
    n_exp, d_model = 2 * u_pk.shape[0], u_pk.shape[1]
    n_tok = ht.shape[1]
    tm = _tile(n_tok, 1024)
    eb = _tile(n_exp, 512)
    n_blocks = n_exp // eb
    once = pl.Buffered(1)
    col = lambda i, e: (0, i)
    nb = eb // nk
    t1 = t1.reshape(PEER_HEADS, n_blocks, nb, n_tok).transpose(1, 0, 2, 3).reshape(n_blocks, PEER_HEADS * nb, n_tok)
    return pl.pallas_call(
        functools.partial(_peer_main_kernel, nk),
        grid=(n_tok // tm, n_blocks),
        in_specs=[
            pl.BlockSpec((eb // 2, d_model), lambda i, e: (e, 0)),
            pl.BlockSpec((d_model, eb), lambda i, e: (0, e)),
            pl.BlockSpec((d_model // 2, tm), col, pipeline_mode=once),
            pl.BlockSpec((1, PEER_HEADS * nb, tm), lambda i, e: (e, 0, i)),
            pl.BlockSpec((s2.shape[0], tm), col, pipeline_mode=once),
            pl.BlockSpec((tau.shape[0], tm), col, pipeline_mode=once),
        ],
        out_specs=pl.BlockSpec((d_model, tm), col),
        out_shape=jax.ShapeDtypeStruct((d_model, n_tok), F32),
        scratch_shapes=[pltpu.VMEM((eb, tm), F32), pltpu.VMEM((eb, tm), BF16)],
        compiler_params=_params(("arbitrary", "arbitrary")),
        name="peer_main",
    )(u_pk, vt_bf16, ht, t1, s2, tau)


def _final_kernel(x_ref, yt_ref, g2_ref, g_ref, o_ref):
    x = x_ref[...] + g2_ref[0] * yt_ref[...].T
    ms = jnp.mean(x * x, axis=-1, keepdims=True)
    o_ref[...] = (x * lax.rsqrt(ms + NORM_EPS)) * g_ref[...]


def _final(x, yt, g2, final_g, seq):
    n_tok, d_model = x.shape
    tm = _tile(seq, 512)
    tiles_per_seq = seq // tm
    return pl.pallas_call(
        _final_kernel,
        grid=(n_tok // tm,),
        in_specs=[
            pl.BlockSpec((tm, d_model), lambda i: (i, 0)),
            pl.BlockSpec((d_model, tm), lambda i: (0, i)),
            pl.BlockSpec((1, 1, d_model), lambda i: (i // tiles_per_seq, 0, 0)),
            pl.BlockSpec((1, d_model), lambda i: (0, 0)),
        ],
        out_specs=pl.BlockSpec((tm, d_model), lambda i: (i, 0)),
        out_shape=jax.ShapeDtypeStruct((n_tok, d_model), F32),
        compiler_params=_params(("arbitrary",)),
        name="final_norm",
    )(x, yt, g2, final_g.reshape(1, d_model))


def kernel(x, c, positions, ada_w, ada_b, norm1_g, w_in, conv_w, lambda_q1, lambda_k1, lambda_q2, lambda_k2,
           subln_g, w_out, norm2_g, peer_wq, peer_keys, peer_u, peer_v, final_g):
    bsz, seq, d_model = x.shape
    depth = ada_w.shape[0]
    n_tok = bsz * seq
    conv_cols = conv_w.shape[2]
    attn_cols = w_out.shape[1] - conv_cols
    q_cols = 3 * conv_cols
    k_cols = q_cols + attn_cols
    v_cols = k_cols + attn_cols
    nk = peer_keys.shape[3]
    sub_dim = attn_cols // ATTN_HEADS // 2
    q_scale = sub_dim ** -0.5 * LOG2E

    rope = _rope_tables(positions)
    mod = _ada_modulation(c, ada_w, ada_b)
    xf = x.reshape(n_tok, d_model)
    prev = None
    for l in range(depth):
        lam_init = 0.8 - 0.6 * math.exp(-0.3 * l)
        sh1, sc1, g1, sh2, sc2, g2 = [m.reshape(bsz, 1, d_model) for m in jnp.split(mod[l], 6, axis=-1)]
        z, xf = _mix_in(xf, prev, norm1_g[l], sc1, sh1, _pack_row_pairs(w_in[l].astype(BF16)), rope, seq,
                        q_cols, k_cols, v_cols, q_scale)
        lam_vecs = jnp.stack([lambda_q1[l], lambda_k1[l], lambda_q2[l], lambda_k2[l]])
        attn = _attention(z, lam_vecs, subln_g[l], bsz, seq, q_cols, k_cols, v_cols, lam_init)
        xf = _mix_out(z, attn, conv_w[l], _pack_row_pairs(w_out[l].astype(BF16)), xf, g1, seq, conv_cols)
        wqt_pk = _pack_row_pairs(peer_wq[l].T.astype(BF16))
        keys_hi, keys_lo = _split_bf16(peer_keys[l])
        ht, t1, s2, tau = _peer_score(xf, norm2_g[l], sc2, sh2, wqt_pk, keys_hi, keys_lo, seq)
        yt = _peer_main(_pack_row_pairs(peer_u[l].astype(BF16)), peer_v[l].T.astype(BF16), ht, t1, s2, tau, nk)
        prev = (yt, g2)
    out = _final(xf, prev[0], prev[1], final_g, seq)
    return out.reshape(bsz, seq, d_model)
```

```python
import functools
import math

import jax
import jax.numpy as jnp
from jax import lax
from jax.experimental import pallas as pl
from jax.experimental.pallas import tpu as pltpu

F32 = jnp.float32
BF16 = jnp.bfloat16

NORM_EPS = 1e-6
ROPE_THETA = 500000.0
ATTN_HEADS = 8
ROPE_DIM = 16
PEER_HEADS = 8
PEER_TOPK = 16
CONV_K = 3

MXU_DIM = 256
LANES = 128
SUBLANES = 8
VMEM_LIMIT = 56 * 1024 * 1024


def _tile(n, pref):
    return pref if n % pref == 0 else n


def _params(sem):
    return pltpu.CompilerParams(dimension_semantics=sem, vmem_limit_bytes=VMEM_LIMIT)


def _ada_kernel(cb_ref, w_ref, b_ref, o_ref):
    d_model = w_ref.shape[1]
    tn = w_ref.shape[2]
    bsz = cb_ref.shape[1]
    groups = tn // LANES

    def body(d8, acc):
        acc = list(acc)
        for r in range(SUBLANES):
            d = d8 * SUBLANES + r
            cv = cb_ref[d]
            cv = cv / (1.0 + jnp.exp(-cv))
            wrow = w_ref[0, pl.ds(d, 1), :]
            for g in range(groups):
                acc[g] = acc[g] + cv * wrow[:, g * LANES:(g + 1) * LANES]
        return tuple(acc)

    acc0 = tuple(jnp.zeros((bsz, LANES), F32) for _ in range(groups))
    acc = lax.fori_loop(0, d_model // SUBLANES, body, acc0)
    for g in range(groups):
        o_ref[0, :, g * LANES:(g + 1) * LANES] = acc[g] + b_ref[0, :, g * LANES:(g + 1) * LANES]


def _ada_modulation(c, ada_w, ada_b):
    depth, d_model, n_out = ada_w.shape
    bsz = c.shape[0]
    tn = _tile(n_out, 1024)
    cb = jnp.broadcast_to(c.T[:, :, None], (d_model, bsz, LANES))
    return pl.pallas_call(
        _ada_kernel,
        grid=(depth, n_out // tn),
        in_specs=[
            pl.BlockSpec((d_model, bsz, LANES), lambda l, j: (0, 0, 0), pipeline_mode=pl.Buffered(1)),
            pl.BlockSpec((1, d_model, tn), lambda l, j: (l, 0, j)),
            pl.BlockSpec((1, 1, tn), lambda l, j: (l, 0, j)),
        ],
        out_specs=pl.BlockSpec((1, bsz, tn), lambda l, j: (l, 0, j)),
        out_shape=jax.ShapeDtypeStruct((depth, bsz, n_out), F32),
        compiler_params=_params(("arbitrary", "arbitrary")),
        name="ada_modulation",
    )(cb, ada_w, ada_b.reshape(depth, 1, n_out))


def _rope_table_kernel(pos_ref, freq_ref, c_ref, s1_ref, s2_ref):
    ang = pos_ref[...].astype(F32) * freq_ref[...]
    lane = lax.broadcasted_iota(jnp.int32, ang.shape, 1) % (LANES // 2)
    half = ROPE_DIM // 2
    cosv = jnp.cos(ang)
    sinv = jnp.sin(ang)
    c_ref[...] = jnp.where(lane < ROPE_DIM, cosv, 1.0)
    s1_ref[...] = jnp.where((lane >= half) & (lane < ROPE_DIM), sinv, 0.0)
    s2_ref[...] = jnp.where(lane < half, -sinv, 0.0)


def _rope_tables(positions):
    n_tok = positions.size
    half = ROPE_DIM // 2
    inv_freq = ROPE_THETA ** (-jnp.arange(0, ROPE_DIM, 2, dtype=F32) / ROPE_DIM)
    sub = jnp.concatenate([inv_freq, inv_freq, jnp.zeros((LANES // 2 - 2 * half,), F32)])
    freq_row = jnp.concatenate([sub, sub]).reshape(1, LANES)
    tm = _tile(n_tok, 2048)
    tab = jax.ShapeDtypeStruct((n_tok, LANES), F32)
    return pl.pallas_call(
        _rope_table_kernel,
        grid=(n_tok // tm,),
        in_specs=[pl.BlockSpec((tm, 1), lambda i: (i, 0)), pl.BlockSpec((1, LANES), lambda i: (0, 0))],
        out_specs=[pl.BlockSpec((tm, LANES), lambda i: (i, 0))] * 3,
        out_shape=[tab, tab, tab],
        compiler_params=_params(("arbitrary",)),
        name="rope_tables",
    )(positions.reshape(n_tok, 1), freq_row)


def _modulated_norm(x, g, sc, sh):
    ms = jnp.mean(x * x, axis=-1, keepdims=True)
    return (x * lax.rsqrt(ms + NORM_EPS)) * g * (1.0 + sc) + sh


def _mix_in_kernel(has_prev, rope_lo, rope_mid, rope_hi, q_scale, *refs):
    if has_prev:
        (x_ref, yt_ref, g2_ref, g_ref, sc_ref, sh_ref, w_ref, rc_ref, rs1_ref, rs2_ref,
         z_ref, xo_ref, h_scr) = refs
    else:
        (x_ref, g_ref, sc_ref, sh_ref, w_ref, rc_ref, rs1_ref, rs2_ref, z_ref, h_scr) = refs
    j = pl.program_id(1)

    @pl.when(j == 0)
    def _():
        x = x_ref[...]
        if has_prev:
            x = x + g2_ref[0] * yt_ref[...].T
            xo_ref[...] = x
        h_scr[...] = _modulated_norm(x, g_ref[...], sc_ref[0], sh_ref[0]).astype(BF16)

    z = jnp.dot(h_scr[...], pltpu.bitcast(w_ref[...], BF16), preferred_element_type=F32)
    is_rope = (j >= rope_lo) & (j < rope_hi)

    @pl.when(jnp.logical_not(is_rope))
    def _():
        z_ref[...] = z.astype(BF16)

    @pl.when(is_rope)
    def _():
        reps = z.shape[1] // LANES
        scale = jnp.where(j < rope_mid, q_scale, 1.0).astype(F32)
        zr_prev = pltpu.roll(z, ROPE_DIM // 2, axis=1)
        zr_next = pltpu.roll(z, z.shape[1] - ROPE_DIM // 2, axis=1)
        for r in range(reps):
            sl = slice(r * LANES, (r + 1) * LANES)
            out = z[:, sl] * rc_ref[...] + zr_prev[:, sl] * rs1_ref[...] + zr_next[:, sl] * rs2_ref[...]
            z_ref[:, sl] = (out * scale).astype(BF16)


def _mix_in(x, prev, norm_g, sc, sh, w_pk, rope, seq, q_cols, k_cols, v_cols, q_scale):
    n_tok, d_model = x.shape
    n_out = w_pk.shape[1]
    tm = _tile(seq, 512)
    tn = _tile(q_cols, 1024)
    assert q_cols % tn == 0 and k_cols % tn == 0 and v_cols % tn == 0
    tiles_per_seq = seq // tm
    has_prev = prev is not None
    row = lambda i, j: (i, 0)
    per_batch = lambda i, j: (i // tiles_per_seq, 0, 0)
    fixed = lambda i, j: (0, 0)
    in_specs = [pl.BlockSpec((tm, d_model), row)]
    args = [x]
    if has_prev:
        yt, g2 = prev
        in_specs += [pl.BlockSpec((d_model, tm), lambda i, j: (0, i)), pl.BlockSpec((1, 1, d_model), per_batch)]
        args += [yt, g2]
    in_specs += [
        pl.BlockSpec((1, d_model), fixed),
        pl.BlockSpec((1, 1, d_model), per_batch),
        pl.BlockSpec((1, 1, d_model), per_batch),
        pl.BlockSpec((d_model // 2, tn), lambda i, j: (0, j)),
        pl.BlockSpec((tm, LANES), row),
        pl.BlockSpec((tm, LANES), row),
        pl.BlockSpec((tm, LANES), row),
    ]
    args += [norm_g.reshape(1, d_model), sc, sh, w_pk, *rope]
    out_specs = [pl.BlockSpec((tm, tn), lambda i, j: (i, j))]
    out_shape = [jax.ShapeDtypeStruct((n_tok, n_out), BF16)]
    if has_prev:
        out_specs.append(pl.BlockSpec((tm, d_model), row))
        out_shape.append(jax.ShapeDtypeStruct((n_tok, d_model), F32))
    kern = functools.partial(_mix_in_kernel, has_prev, q_cols // tn, k_cols // tn, v_cols // tn, q_scale)
    outs = pl.pallas_call(
        kern,
        grid=(n_tok // tm, n_out // tn),
        in_specs=in_specs,
        out_specs=out_specs,
        out_shape=out_shape,
        scratch_shapes=[pltpu.VMEM((tm, d_model), BF16)],
        compiler_params=_params(("arbitrary", "arbitrary")),
        name="mix_in",
    )(*args)
    return (outs[0], outs[1]) if has_prev else (outs[0], x)


def _attention_kernel(lam_init, q_ref, k_ref, v_ref, lam_ref, g_ref, o_ref):
    tq = q_ref.shape[0]
    q = q_ref[...]
    lane = lax.broadcasted_iota(jnp.int32, q.shape, 1)
    zero = jnp.zeros_like(q)
    qq = jnp.concatenate([jnp.where(lane < LANES // 2, q, zero), jnp.where(lane >= LANES // 2, q, zero)], axis=0)
    s = lax.dot_general(qq, k_ref[...], (((1,), (1,)), ((), ())), preferred_element_type=F32)
    m = jnp.max(s, axis=1, keepdims=True)
    p = jnp.exp2(s - m)
    l = jnp.sum(p, axis=1, keepdims=True)
    lv = lam_ref[...]
    lam = (jnp.exp(jnp.sum(lv[0:1] * lv[1:2], axis=1, keepdims=True))
           - jnp.exp(jnp.sum(lv[2:3] * lv[3:4], axis=1, keepdims=True)) + lam_init)
    r = lam * l[:tq] / l[tq:]
    a = (p[:tq] - p[tq:] * r).astype(BF16)
    o = jnp.dot(a, v_ref[...], preferred_element_type=F32) * (1.0 / l[:tq])
    ms = jnp.mean(o * o, axis=-1, keepdims=True)
    o_ref[...] = ((o * lax.rsqrt(ms + NORM_EPS)) * g_ref[...] * (1.0 - lam_init)).astype(BF16)


def _attention(z, lam_vecs, subln_g, bsz, seq, q_cols, k_cols, v_cols, lam_init):
    n_tok = z.shape[0]
    hd = LANES
    tq = _tile(seq, 256)
    nq = seq // tq
    qb, kb, vb = q_cols // hd, k_cols // hd, v_cols // hd
    return pl.pallas_call(
        functools.partial(_attention_kernel, lam_init),
        grid=(bsz, ATTN_HEADS, nq),
        in_specs=[
            pl.BlockSpec((tq, hd), lambda b, h, i: (b * nq + i, qb + h)),
            pl.BlockSpec((seq, hd), lambda b, h, i: (b, kb + h)),
            pl.BlockSpec((seq, hd), lambda b, h, i: (b, vb + h)),
            pl.BlockSpec(lam_vecs.shape, lambda b, h, i: (0, 0)),
            pl.BlockSpec((1, hd), lambda b, h, i: (0, 0)),
        ],
        out_specs=pl.BlockSpec((tq, hd), lambda b, h, i: (b * nq + i, h)),
        out_shape=jax.ShapeDtypeStruct((n_tok, ATTN_HEADS * hd), BF16),
        compiler_params=_params(("arbitrary", "arbitrary", "arbitrary")),
        name="diff_attention",
    )(z, z, z, lam_vecs, subln_g.reshape(1, hd))


def _mix_out_kernel(tiles_per_seq, gb_ref, gc_ref, hv_ref, gcp_ref, hvp_ref, gcn_ref, hvn_ref, at_ref,
                    cw_ref, w_ref, x_ref, g1_ref, o_ref):
    i = pl.program_id(0)
    tm = gc_ref.shape[0]
    cw = gc_ref.shape[1]
    zc = gc_ref[...].astype(F32) * hv_ref[...].astype(F32)
    first = (i % tiles_per_seq) == 0
    last = (i % tiles_per_seq) == tiles_per_seq - 1
    prev_row = gcp_ref[SUBLANES - 1:SUBLANES, :].astype(F32) * hvp_ref[SUBLANES - 1:SUBLANES, :].astype(F32)
    next_row = gcn_ref[0:1, :].astype(F32) * hvn_ref[0:1, :].astype(F32)
    prev_row = jnp.where(first, 0.0, prev_row)
    next_row = jnp.where(last, 0.0, next_row)
    rows = lax.broadcasted_iota(jnp.int32, zc.shape, 0)
    z_prev = jnp.where(rows == 0, prev_row, pltpu.roll(zc, 1, axis=0))
    z_next = jnp.where(rows == tm - 1, next_row, pltpu.roll(zc, tm - 1, axis=0))
    y = cw_ref[0:1, :] * z_prev + cw_ref[1:2, :] * zc + cw_ref[2:3, :] * z_next
    conv = (gb_ref[...].astype(F32) * y).astype(BF16)
    acc = jnp.dot(conv, pltpu.bitcast(w_ref[:cw // 2, :], BF16), preferred_element_type=F32)
    acc = acc + jnp.dot(at_ref[...], pltpu.bitcast(w_ref[cw // 2:, :], BF16), preferred_element_type=F32)
    o_ref[...] = x_ref[...] + g1_ref[0] * acc


def _mix_out(z, attn, conv_w, w_pk, x, g1, seq, conv_cols):
    n_tok, d_model = x.shape
    cw = conv_cols
    aw = attn.shape[1]
    tm = _tile(seq, 512)
    tiles_per_seq = seq // tm
    hb = tm // SUBLANES
    last_hb = n_tok // SUBLANES - 1
    row = lambda i: (i, 0)
    prev_blk = lambda i: (jnp.maximum(i * hb - 1, 0), 1)
    prev_blk2 = lambda i: (jnp.maximum(i * hb - 1, 0), 2)
    next_blk = lambda i: (jnp.minimum((i + 1) * hb, last_hb), 1)
    next_blk2 = lambda i: (jnp.minimum((i + 1) * hb, last_hb), 2)
    return pl.pallas_call(
        functools.partial(_mix_out_kernel, tiles_per_seq),
        grid=(n_tok // tm,),
        in_specs=[
            pl.BlockSpec((tm, cw), lambda i: (i, 0)),
            pl.BlockSpec((tm, cw), lambda i: (i, 1)),
            pl.BlockSpec((tm, cw), lambda i: (i, 2)),
            pl.BlockSpec((SUBLANES, cw), prev_blk),
            pl.BlockSpec((SUBLANES, cw), prev_blk2),
            pl.BlockSpec((SUBLANES, cw), next_blk),
            pl.BlockSpec((SUBLANES, cw), next_blk2),
            pl.BlockSpec((tm, aw), row),
            pl.BlockSpec((CONV_K, cw), lambda i: (0, 0)),
            pl.BlockSpec(((cw + aw) // 2, d_model), lambda i: (0, 0), pipeline_mode=pl.Buffered(1)),
            pl.BlockSpec((tm, d_model), row),
            pl.BlockSpec((1, 1, d_model), lambda i: (i // tiles_per_seq, 0, 0)),
        ],
        out_specs=pl.BlockSpec((tm, d_model), row),
        out_shape=jax.ShapeDtypeStruct((n_tok, d_model), F32),
        compiler_params=_params(("arbitrary",)),
        name="mix_out",
    )(z, z, z, z, z, z, z, attn, conv_w, w_pk, x, g1)


def _split_bf16(a):
    hi = a.astype(BF16)
    lo = (a - hi.astype(F32)).astype(BF16)
    return hi, lo


def _dot3(a_hi, a_lo, b_hi, b_lo):
    d = functools.partial(jnp.dot, preferred_element_type=F32)
    return d(a_hi, b_hi) + (d(a_hi, b_lo) + d(a_lo, b_hi))


def _top_values(s, count):
    vals = []
    work = s
    for k in range(count):
        m = jnp.max(work, axis=0, keepdims=True)
        vals.append(m)
        if k + 1 < count:
            work = jnp.where(work == m, -jnp.inf, work)
    return vals


def _stack_rows(rows):
    n = len(rows)
    idx = lax.broadcasted_iota(jnp.int32, (n, rows[0].shape[1]), 0)
    out = jnp.broadcast_to(rows[0], idx.shape)
    for k in range(1, n):
        out = jnp.where(idx == k, rows[k], out)
    return out


def _peer_score_kernel(x_ref, g_ref, sc_ref, sh_ref, wq_ref, kh_ref, kl_ref,
                       ht_ref, t1_ref, s2_ref, tau_ref):
    nk = kh_ref.shape[2]
    h = _modulated_norm(x_ref[...], g_ref[...], sc_ref[0], sh_ref[0])
    ht = h.T
    ht_hi = ht.astype(BF16)
    ht_ref[...] = pltpu.bitcast(ht_hi, jnp.uint32)
    qt = jnp.dot(pltpu.bitcast(wq_ref[...], BF16), ht_hi, preferred_element_type=F32)
    half = kh_ref.shape[3]
    k = PEER_TOPK
    for hd in range(PEER_HEADS):
        st = []
        for c in range(2):
            r0 = (hd * 2 + c) * half
            q_hi, q_lo = _split_bf16(qt[r0:r0 + half, :])
            st.append(_dot3(kh_ref[hd, c], kl_ref[hd, c], q_hi, q_lo))
        a = _top_values(st[0], k)
        b = _top_values(st[1], k)
        b_all = _stack_rows(b)
        a_tail = _stack_rows(a[k // 2:])
        cand = [a[0] + b_all]
        cand += [a[p] + b_all[:k // 2] for p in range(1, k // 2)]
        cand.append(a_tail + b[0])
        cand = jnp.concatenate(cand, axis=0)
        top = _top_values(cand, k + 1)
        tau = 0.5 * (top[k - 1] + top[k])
        z = jnp.sum(jnp.where(cand > tau, jnp.exp(cand - top[0]), 0.0), axis=0, keepdims=True)
        off = top[0] + jnp.log(z)
        t1_ref[hd * nk:(hd + 1) * nk, :] = (st[0] - off) * LOG2E
        s2_ref[hd * nk:(hd + 1) * nk, :] = st[1] * LOG2E
        tau_ref[hd:hd + 1, :] = (tau - off) * LOG2E


def _peer_score(x, norm_g, sc, sh, wqt_pk, keys_hi, keys_lo, seq):
    n_tok, d_model = x.shape
    nk = keys_hi.shape[2]
    tm = _tile(seq, 256)
    tiles_per_seq = seq // tm
    per_batch = lambda i: (i // tiles_per_seq, 0, 0)
    once = pl.Buffered(1)
    col = lambda i: (0, i)
    return pl.pallas_call(
        _peer_score_kernel,
        grid=(n_tok // tm,),
        in_specs=[
            pl.BlockSpec((tm, d_model), lambda i: (i, 0)),
            pl.BlockSpec((1, d_model), lambda i: (0, 0)),
            pl.BlockSpec((1, 1, d_model), per_batch),
            pl.BlockSpec((1, 1, d_model), per_batch),
            pl.BlockSpec(wqt_pk.shape, lambda i: (0, 0), pipeline_mode=once),
            pl.BlockSpec(keys_hi.shape, lambda i: (0, 0, 0, 0), pipeline_mode=once),
            pl.BlockSpec(keys_lo.shape, lambda i: (0, 0, 0, 0), pipeline_mode=once),
        ],
        out_specs=[
            pl.BlockSpec((d_model // 2, tm), col),
            pl.BlockSpec((PEER_HEADS * nk, tm), col),
            pl.BlockSpec((PEER_HEADS * nk, tm), col),
            pl.BlockSpec((PEER_HEADS, tm), col),
        ],
        out_shape=[
            jax.ShapeDtypeStruct((d_model // 2, n_tok), jnp.uint32),
            jax.ShapeDtypeStruct((PEER_HEADS * nk, n_tok), F32),
            jax.ShapeDtypeStruct((PEER_HEADS * nk, n_tok), F32),
            jax.ShapeDtypeStruct((PEER_HEADS, n_tok), F32),
        ],
        compiler_params=_params(("arbitrary",)),
        name="peer_score",
    )(x, norm_g.reshape(1, d_model), sc, sh, wqt_pk, keys_hi, keys_lo)


GATE_ROWS = 32
PIPE_COLS = 256
LOG2E = 1.0 / math.log(2.0)


def _peer_main_kernel(nk, u_ref, vt_ref, ht_ref, *refs):
    t1_refs = refs[:PEER_HEADS]
    s2_ref, tau_ref, o_ref, a_scr, ag_scr = refs[PEER_HEADS:]
    e = pl.program_id(1)
    eb, tm = a_scr.shape
    d_model = o_ref.shape[0]
    nb = eb // nk
    groups = GATE_ROWS // SUBLANES
    n_chunks = tm // PIPE_COLS
    pieces = d_model // MXU_DIM
    b_rows = eb // pieces
    c_rows = d_model // pieces
    assert nk % b_rows == 0 and b_rows % GATE_ROWS == 0

    @pl.when(e == 0)
    def _():
        o_ref[...] = jnp.zeros_like(o_ref)

    def wide(cc):
        return slice(cc * PIPE_COLS, (cc + 1) * PIPE_COLS)

    def stage_a(cc, p, acc):
        ks = slice(p * MXU_DIM, (p + 1) * MXU_DIM)
        part = jnp.dot(pltpu.bitcast(u_ref[:, ks], BF16), pltpu.bitcast(ht_ref[p * MXU_DIM // 2:(p + 1) * MXU_DIM // 2, wide(cc)], BF16),
                       preferred_element_type=F32)
        acc = part if acc is None else acc + part
        if p == pieces - 1:
            a_scr[:, wide(cc)] = acc
        return acc

    def stage_b(cc, p):
        ii, r_base = divmod(p * b_rows, nk)
        for sub in range(PIPE_COLS // LANES):
            cols = slice(cc * PIPE_COLS + sub * LANES, cc * PIPE_COLS + (sub + 1) * LANES)
            for r0 in range(r_base, r_base + b_rows, GATE_ROWS):
                gate = [None] * groups
                for hd in range(PEER_HEADS):
                    tau_b = jnp.broadcast_to(tau_ref[hd:hd + 1, cols], (SUBLANES, LANES))
                    t1_b = jnp.broadcast_to(t1_refs[hd][ii:ii + 1, cols], (SUBLANES, LANES))
                    for g in range(groups):
                        row = hd * nk + r0 + g * SUBLANES
                        sm = t1_b + s2_ref[row:row + SUBLANES, cols]
                        w = jnp.where(sm > tau_b, jnp.exp2(sm), 0.0)
                        gate[g] = w if gate[g] is None else gate[g] + w
                for g in range(groups):
                    rows = slice(ii * nk + r0 + g * SUBLANES, ii * nk + r0 + (g + 1) * SUBLANES)
                    act = a_scr[rows, cols]
                    gel = 0.5 * act * (1.0 + lax.erf(act * (1.0 / math.sqrt(2.0))))
                    ag_scr[rows, cols] = (gel * gate[g]).astype(BF16)

    def stage_c(cc, p):
        rows = slice(p * c_rows, (p + 1) * c_rows)
        vt_rows = pltpu.bitcast(vt_ref[p * c_rows // 2:(p + 1) * c_rows // 2, :], BF16)
        o_ref[rows, wide(cc)] += jnp.dot(vt_rows, ag_scr[:, wide(cc)], preferred_element_type=F32)

    for slot in range(n_chunks + 2):
        acc = None
        for p in range(pieces):
            if slot < n_chunks:
                acc = stage_a(slot, p, acc)
            if 0 <= slot - 1 < n_chunks:
                stage_b(slot - 1, p)
            if 0 <= slot - 2 < n_chunks:
                stage_c(slot - 2, p)


def _pack_kernel(transpose, x_ref, o_ref):
    x = x_ref[...]
    if transpose:
        x = x.T
    o_ref[...] = pltpu.bitcast(x.astype(BF16), jnp.uint32)


def _pack_weight(w, transpose=False):
    r, n = w.shape
    br = _tile(r, 512)
    if transpose:
        out_spec = pl.BlockSpec((n // 2, br), lambda i: (0, i))
        out_shape = jax.ShapeDtypeStruct((n // 2, r), jnp.uint32)
    else:
        out_spec = pl.BlockSpec((br // 2, n), lambda i: (i, 0))
        out_shape = jax.ShapeDtypeStruct((r // 2, n), jnp.uint32)
    return pl.pallas_call(
        functools.partial(_pack_kernel, transpose),
        grid=(r // br,),
        in_specs=[pl.BlockSpec((br, n), lambda i: (i, 0))],
        out_specs=out_spec,
        out_shape=out_shape,
        compiler_params=_params(("arbitrary",)),
        name="pack_weight",
    )(w)


def _peer_main(u_pk, vt_pk, ht, t1, s2, tau, nk):
    n_exp, d_model = 2 * u_pk.shape[0], u_pk.shape[1]
    n_tok = ht.shape[1]
    tm = _tile(n_tok, 1024)
    eb = _tile(n_exp, 1024)
    n_blocks = n_exp // eb
    once = pl.Buffered(1)
    col = lambda i, e: (0, i)
    nb = eb // nk
    t1_specs = [pl.BlockSpec((nb, tm), functools.partial(lambda hd, i, e: (hd * n_blocks + e, i), hd))
                for hd in range(PEER_HEADS)]
    return pl.pallas_call(
        functools.partial(_peer_main_kernel, nk),
        grid=(n_tok // tm, n_blocks),
        in_specs=[
            pl.BlockSpec((eb // 2, d_model), lambda i, e: (e, 0)),
            pl.BlockSpec((d_model // 2, eb), lambda i, e: (0, e)),
            pl.BlockSpec((d_model // 2, tm), col, pipeline_mode=once),
            *t1_specs,
            pl.BlockSpec((s2.shape[0], tm), col, pipeline_mode=once),
            pl.BlockSpec((tau.shape[0], tm), col, pipeline_mode=once),
        ],
        out_specs=pl.BlockSpec((d_model, tm), col),
        out_shape=jax.ShapeDtypeStruct((d_model, n_tok), F32),
        scratch_shapes=[pltpu.VMEM((eb, tm), F32), pltpu.VMEM((eb, tm), BF16)],
        compiler_params=_params(("arbitrary", "arbitrary")),
        name="peer_main",
    )(u_pk, vt_pk, ht, *([t1] * PEER_HEADS), s2, tau)


def _final_kernel(x_ref, yt_ref, g2_ref, g_ref, o_ref):
    x = x_ref[...] + g2_ref[0] * yt_ref[...].T
    ms = jnp.mean(x * x, axis=-1, keepdims=True)
    o_ref[...] = (x * lax.rsqrt(ms + NORM_EPS)) * g_ref[...]


def _final(x, yt, g2, final_g, seq):
    n_tok, d_model = x.shape
    tm = _tile(seq, 512)
    tiles_per_seq = seq // tm
    return pl.pallas_call(
        _final_kernel,
        grid=(n_tok // tm,),
        in_specs=[
            pl.BlockSpec((tm, d_model), lambda i: (i, 0)),
            pl.BlockSpec((d_model, tm), lambda i: (0, i)),
            pl.BlockSpec((1, 1, d_model), lambda i: (i // tiles_per_seq, 0, 0)),
            pl.BlockSpec((1, d_model), lambda i: (0, 0)),
        ],
        out_specs=pl.BlockSpec((tm, d_model), lambda i: (i, 0)),
        out_shape=jax.ShapeDtypeStruct((n_tok, d_model), F32),
        compiler_params=_params(("arbitrary",)),
        name="final_norm",
    )(x, yt, g2, final_g.reshape(1, d_model))


def kernel(x, c, positions, ada_w, ada_b, norm1_g, w_in, conv_w, lambda_q1, lambda_k1, lambda_q2, lambda_k2,
           subln_g, w_out, norm2_g, peer_wq, peer_keys, peer_u, peer_v, final_g):
    bsz, seq, d_model = x.shape
    depth = ada_w.shape[0]
    n_tok = bsz * seq
    conv_cols = conv_w.shape[2]
    attn_cols = w_out.shape[1] - conv_cols
    q_cols = 3 * conv_cols
    k_cols = q_cols + attn_cols
    v_cols = k_cols + attn_cols
    nk = peer_keys.shape[3]
    sub_dim = attn_cols // ATTN_HEADS // 2
    q_scale = sub_dim ** -0.5 * LOG2E

    rope = _rope_tables(positions)
    mod = _ada_modulation(c, ada_w, ada_b)
    xf = x.reshape(n_tok, d_model)
    prev = None
    for l in range(depth):
        lam_init = 0.8 - 0.6 * math.exp(-0.3 * l)
        sh1, sc1, g1, sh2, sc2, g2 = [m.reshape(bsz, 1, d_model) for m in jnp.split(mod[l], 6, axis=-1)]
        z, xf = _mix_in(xf, prev, norm1_g[l], sc1, sh1, _pack_weight(w_in[l]), rope, seq,
                        q_cols, k_cols, v_cols, q_scale)
        lam_vecs = jnp.stack([lambda_q1[l], lambda_k1[l], lambda_q2[l], lambda_k2[l]])
        attn = _attention(z, lam_vecs, subln_g[l], bsz, seq, q_cols, k_cols, v_cols, lam_init)
        xf = _mix_out(z, attn, conv_w[l], _pack_weight(w_out[l]), xf, g1, seq, conv_cols)
        wqt_pk = _pack_weight(peer_wq[l], transpose=True)
        keys_hi, keys_lo = _split_bf16(peer_keys[l])
        ht, t1, s2, tau = _peer_score(xf, norm2_g[l], sc2, sh2, wqt_pk, keys_hi, keys_lo, seq)
        yt = _peer_main(_pack_weight(peer_u[l]), _pack_weight(peer_v[l], transpose=True), ht, t1, s2, tau, nk)
        prev = (yt, g2)
    out = _final(xf, prev[0], prev[1], final_g, seq)
    return out.reshape(bsz, seq, d_model)
```

```python
import functools
import math

import jax
import jax.numpy as jnp
from jax import lax
from jax.experimental import pallas as pl
from jax.experimental.pallas import tpu as pltpu

F32 = jnp.float32
BF16 = jnp.bfloat16

NORM_EPS = 1e-6
ROPE_THETA = 500000.0
ATTN_HEADS = 8
ROPE_DIM = 16
PEER_HEADS = 8
PEER_TOPK = 16
CONV_K = 3

MXU_DIM = 256
LANES = 128
SUBLANES = 8
VMEM_LIMIT = 56 * 1024 * 1024


def _tile(n, pref):
    return pref if n % pref == 0 else n


def _params(sem):
    return pltpu.CompilerParams(dimension_semantics=sem, vmem_limit_bytes=VMEM_LIMIT)


def _ada_kernel(cb_ref, w_ref, b_ref, o_ref, act_scr):
    d_model = w_ref.shape[1]
    tn = w_ref.shape[2]
    bsz = cb_ref.shape[1]
    groups = tn // LANES

    @pl.when((pl.program_id(0) == 0) & (pl.program_id(1) == 0))
    def _():
        cv = cb_ref[...]
        act_scr[...] = cv / (1.0 + jnp.exp(-cv))

    def body(d8, acc):
        acc = list(acc)
        for r in range(SUBLANES):
            d = d8 * SUBLANES + r
            cv = act_scr[d]
            wrow = w_ref[0, pl.ds(d, 1), :]
            for g in range(groups):
                acc[g] = acc[g] + cv * wrow[:, g * LANES:(g + 1) * LANES]
        return tuple(acc)

    acc0 = tuple(jnp.zeros((bsz, LANES), F32) for _ in range(groups))
    acc = lax.fori_loop(0, d_model // SUBLANES, body, acc0)
    for g in range(groups):
        o_ref[0, :, g * LANES:(g + 1) * LANES] = acc[g] + b_ref[0, :, g * LANES:(g + 1) * LANES]


def _ada_modulation(c, ada_w, ada_b):
    depth, d_model, n_out = ada_w.shape
    bsz = c.shape[0]
    tn = _tile(n_out, 1024)
    cb = jnp.broadcast_to(c.T[:, :, None], (d_model, bsz, LANES))
    return pl.pallas_call(
        _ada_kernel,
        grid=(depth, n_out // tn),
        in_specs=[
            pl.BlockSpec((d_model, bsz, LANES), lambda l, j: (0, 0, 0), pipeline_mode=pl.Buffered(1)),
            pl.BlockSpec((1, d_model, tn), lambda l, j: (l, 0, j)),
            pl.BlockSpec((1, 1, tn), lambda l, j: (l, 0, j)),
        ],
        out_specs=pl.BlockSpec((1, bsz, tn), lambda l, j: (l, 0, j)),
        out_shape=jax.ShapeDtypeStruct((depth, bsz, n_out), F32),
        scratch_shapes=[pltpu.VMEM((d_model, bsz, LANES), F32)],
        compiler_params=_params(("arbitrary", "arbitrary")),
        name="ada_modulation",
    )(cb, ada_w, ada_b.reshape(depth, 1, n_out))


def _rope_table_kernel(pos_ref, freq_ref, c_ref, s1_ref, s2_ref):
    ang = pos_ref[...].astype(F32) * freq_ref[...]
    lane = lax.broadcasted_iota(jnp.int32, ang.shape, 1) % (LANES // 2)
    half = ROPE_DIM // 2
    cosv = jnp.cos(ang)
    sinv = jnp.sin(ang)
    c_ref[...] = jnp.where(lane < ROPE_DIM, cosv, 1.0)
    s1_ref[...] = jnp.where((lane >= half) & (lane < ROPE_DIM), sinv, 0.0)
    s2_ref[...] = jnp.where(lane < half, -sinv, 0.0)


def _rope_tables(positions):
    n_tok = positions.size
    half = ROPE_DIM // 2
    inv_freq = ROPE_THETA ** (-jnp.arange(0, ROPE_DIM, 2, dtype=F32) / ROPE_DIM)
    sub = jnp.concatenate([inv_freq, inv_freq, jnp.zeros((LANES // 2 - 2 * half,), F32)])
    freq_row = jnp.concatenate([sub, sub]).reshape(1, LANES)
    tm = _tile(n_tok, 2048)
    tab = jax.ShapeDtypeStruct((n_tok, LANES), F32)
    return pl.pallas_call(
        _rope_table_kernel,
        grid=(n_tok // tm,),
        in_specs=[pl.BlockSpec((tm, 1), lambda i: (i, 0)), pl.BlockSpec((1, LANES), lambda i: (0, 0))],
        out_specs=[pl.BlockSpec((tm, LANES), lambda i: (i, 0))] * 3,
        out_shape=[tab, tab, tab],
        compiler_params=_params(("arbitrary",)),
        name="rope_tables",
    )(positions.reshape(n_tok, 1), freq_row)


def _modulated_norm(x, g, sc, sh):
    ms = jnp.mean(x * x, axis=-1, keepdims=True)
    return (x * lax.rsqrt(ms + NORM_EPS)) * g * (1.0 + sc) + sh


def _mix_in_kernel(has_prev, rope_lo, rope_mid, rope_hi, q_scale, *refs):
    if has_prev:
        (x_ref, yt_ref, g2_ref, g_ref, sc_ref, sh_ref, w_ref, rc_ref, rs1_ref, rs2_ref,
         z_ref, xo_ref, h_scr) = refs
    else:
        (x_ref, g_ref, sc_ref, sh_ref, w_ref, rc_ref, rs1_ref, rs2_ref, z_ref, h_scr) = refs
    j = pl.program_id(1)

    @pl.when(j == 0)
    def _():
        x = x_ref[...]
        if has_prev:
            x = x + g2_ref[0] * yt_ref[...].T
            xo_ref[...] = x
        h_scr[...] = _modulated_norm(x, g_ref[...], sc_ref[0], sh_ref[0]).astype(BF16)

    z = jnp.dot(h_scr[...], pltpu.bitcast(w_ref[...], BF16), preferred_element_type=F32)
    is_rope = (j >= rope_lo) & (j < rope_hi)

    @pl.when(jnp.logical_not(is_rope))
    def _():
        z_ref[...] = z.astype(BF16)

    @pl.when(is_rope)
    def _():
        reps = z.shape[1] // LANES
        scale = jnp.where(j < rope_mid, q_scale, 1.0).astype(F32)
        zr_prev = pltpu.roll(z, ROPE_DIM // 2, axis=1)
        zr_next = pltpu.roll(z, z.shape[1] - ROPE_DIM // 2, axis=1)
        for r in range(reps):
            sl = slice(r * LANES, (r + 1) * LANES)
            out = z[:, sl] * rc_ref[...] + zr_prev[:, sl] * rs1_ref[...] + zr_next[:, sl] * rs2_ref[...]
            z_ref[:, sl] = (out * scale).astype(BF16)


def _mix_in(x, prev, norm_g, sc, sh, w_pk, rope, seq, q_cols, k_cols, v_cols, q_scale):
    n_tok, d_model = x.shape
    n_out = w_pk.shape[1]
    tm = _tile(seq, 512)
    tn = _tile(q_cols, 1024)
    assert q_cols % tn == 0 and k_cols % tn == 0 and v_cols % tn == 0
    tiles_per_seq = seq // tm
    has_prev = prev is not None
    row = lambda i, j: (i, 0)
    per_batch = lambda i, j: (i // tiles_per_seq, 0, 0)
    fixed = lambda i, j: (0, 0)
    in_specs = [pl.BlockSpec((tm, d_model), row)]
    args = [x]
    if has_prev:
        yt, g2 = prev
        in_specs += [pl.BlockSpec((d_model, tm), lambda i, j: (0, i)), pl.BlockSpec((1, 1, d_model), per_batch)]
        args += [yt, g2]
    in_specs += [
        pl.BlockSpec((1, d_model), fixed),
        pl.BlockSpec((1, 1, d_model), per_batch),
        pl.BlockSpec((1, 1, d_model), per_batch),
        pl.BlockSpec((d_model // 2, tn), lambda i, j: (0, j)),
        pl.BlockSpec((tm, LANES), row),
        pl.BlockSpec((tm, LANES), row),
        pl.BlockSpec((tm, LANES), row),
    ]
    args += [norm_g.reshape(1, d_model), sc, sh, w_pk, *rope]
    out_specs = [pl.BlockSpec((tm, tn), lambda i, j: (i, j))]
    out_shape = [jax.ShapeDtypeStruct((n_tok, n_out), BF16)]
    if has_prev:
        out_specs.append(pl.BlockSpec((tm, d_model), row))
        out_shape.append(jax.ShapeDtypeStruct((n_tok, d_model), F32))
    kern = functools.partial(_mix_in_kernel, has_prev, q_cols // tn, k_cols // tn, v_cols // tn, q_scale)
    outs = pl.pallas_call(
        kern,
        grid=(n_tok // tm, n_out // tn),
        in_specs=in_specs,
        out_specs=out_specs,
        out_shape=out_shape,
        scratch_shapes=[pltpu.VMEM((tm, d_model), BF16)],
        compiler_params=_params(("arbitrary", "arbitrary")),
        name="mix_in",
    )(*args)
    return (outs[0], outs[1]) if has_prev else (outs[0], x)


def _attention_kernel(lam_init, q_ref, k_ref, v_ref, lam_ref, g_ref, o_ref):
    tq = q_ref.shape[0]
    q = q_ref[...]
    lane = lax.broadcasted_iota(jnp.int32, q.shape, 1)
    zero = jnp.zeros_like(q)
    qq = jnp.concatenate([jnp.where(lane < LANES // 2, q, zero), jnp.where(lane >= LANES // 2, q, zero)], axis=0)
    s = lax.dot_general(qq, k_ref[...], (((1,), (1,)), ((), ())), preferred_element_type=F32)
    m = jnp.max(s, axis=1, keepdims=True)
    p = jnp.exp2(s - m)
    l = jnp.sum(p, axis=1, keepdims=True)
    lv = lam_ref[...]
    lam = (jnp.exp(jnp.sum(lv[0:1] * lv[1:2], axis=1, keepdims=True))
           - jnp.exp(jnp.sum(lv[2:3] * lv[3:4], axis=1, keepdims=True)) + lam_init)
    r = lam * l[:tq] / l[tq:]
    a = (p[:tq] - p[tq:] * r).astype(BF16)
    o = jnp.dot(a, v_ref[...], preferred_element_type=F32) * (1.0 / l[:tq])
    ms = jnp.mean(o * o, axis=-1, keepdims=True)
    o_ref[...] = ((o * lax.rsqrt(ms + NORM_EPS)) * g_ref[...] * (1.0 - lam_init)).astype(BF16)


def _attention(z, lam_vecs, subln_g, bsz, seq, q_cols, k_cols, v_cols, lam_init):
    n_tok = z.shape[0]
    hd = LANES
    tq = _tile(seq, 256)
    nq = seq // tq
    qb, kb, vb = q_cols // hd, k_cols // hd, v_cols // hd
    return pl.pallas_call(
        functools.partial(_attention_kernel, lam_init),
        grid=(bsz, ATTN_HEADS, nq),
        in_specs=[
            pl.BlockSpec((tq, hd), lambda b, h, i: (b * nq + i, qb + h)),
            pl.BlockSpec((seq, hd), lambda b, h, i: (b, kb + h)),
            pl.BlockSpec((seq, hd), lambda b, h, i: (b, vb + h)),
            pl.BlockSpec(lam_vecs.shape, lambda b, h, i: (0, 0)),
            pl.BlockSpec((1, hd), lambda b, h, i: (0, 0)),
        ],
        out_specs=pl.BlockSpec((tq, hd), lambda b, h, i: (b * nq + i, h)),
        out_shape=jax.ShapeDtypeStruct((n_tok, ATTN_HEADS * hd), BF16),
        compiler_params=_params(("arbitrary", "arbitrary", "arbitrary")),
        name="diff_attention",
    )(z, z, z, lam_vecs, subln_g.reshape(1, hd))


def _mix_out_kernel(tiles_per_seq, gb_ref, gc_ref, hv_ref, gcp_ref, hvp_ref, gcn_ref, hvn_ref, at_ref,
                    cw_ref, w_ref, x_ref, g1_ref, o_ref):
    i = pl.program_id(0)
    tm = gc_ref.shape[0]
    cw = gc_ref.shape[1]
    zc = gc_ref[...].astype(F32) * hv_ref[...].astype(F32)
    first = (i % tiles_per_seq) == 0
    last = (i % tiles_per_seq) == tiles_per_seq - 1
    prev_row = gcp_ref[SUBLANES - 1:SUBLANES, :].astype(F32) * hvp_ref[SUBLANES - 1:SUBLANES, :].astype(F32)
    next_row = gcn_ref[0:1, :].astype(F32) * hvn_ref[0:1, :].astype(F32)
    prev_row = jnp.where(first, 0.0, prev_row)
    next_row = jnp.where(last, 0.0, next_row)
    rows = lax.broadcasted_iota(jnp.int32, zc.shape, 0)
    z_prev = jnp.where(rows == 0, prev_row, pltpu.roll(zc, 1, axis=0))
    z_next = jnp.where(rows == tm - 1, next_row, pltpu.roll(zc, tm - 1, axis=0))
    y = cw_ref[0:1, :] * z_prev + cw_ref[1:2, :] * zc + cw_ref[2:3, :] * z_next
    conv = (gb_ref[...].astype(F32) * y).astype(BF16)
    acc = jnp.dot(conv, pltpu.bitcast(w_ref[:cw // 2, :], BF16), preferred_element_type=F32)
    acc = acc + jnp.dot(at_ref[...], pltpu.bitcast(w_ref[cw // 2:, :], BF16), preferred_element_type=F32)
    o_ref[...] = x_ref[...] + g1_ref[0] * acc


def _mix_out(z, attn, conv_w, w_pk, x, g1, seq, conv_cols):
    n_tok, d_model = x.shape
    cw = conv_cols
    aw = attn.shape[1]
    tm = _tile(seq, 512)
    tiles_per_seq = seq // tm
    hb = tm // SUBLANES
    last_hb = n_tok // SUBLANES - 1
    row = lambda i: (i, 0)
    prev_blk = lambda i: (jnp.maximum(i * hb - 1, 0), 1)
    prev_blk2 = lambda i: (jnp.maximum(i * hb - 1, 0), 2)
    next_blk = lambda i: (jnp.minimum((i + 1) * hb, last_hb), 1)
    next_blk2 = lambda i: (jnp.minimum((i + 1) * hb, last_hb), 2)
    return pl.pallas_call(
        functools.partial(_mix_out_kernel, tiles_per_seq),
        grid=(n_tok // tm,),
        in_specs=[
            pl.BlockSpec((tm, cw), lambda i: (i, 0)),
            pl.BlockSpec((tm, cw), lambda i: (i, 1)),
            pl.BlockSpec((tm, cw), lambda i: (i, 2)),
            pl.BlockSpec((SUBLANES, cw), prev_blk),
            pl.BlockSpec((SUBLANES, cw), prev_blk2),
            pl.BlockSpec((SUBLANES, cw), next_blk),
            pl.BlockSpec((SUBLANES, cw), next_blk2),
            pl.BlockSpec((tm, aw), row),
            pl.BlockSpec((CONV_K, cw), lambda i: (0, 0)),
            pl.BlockSpec(((cw + aw) // 2, d_model), lambda i: (0, 0), pipeline_mode=pl.Buffered(1)),
            pl.BlockSpec((tm, d_model), row),
            pl.BlockSpec((1, 1, d_model), lambda i: (i // tiles_per_seq, 0, 0)),
        ],
        out_specs=pl.BlockSpec((tm, d_model), row),
        out_shape=jax.ShapeDtypeStruct((n_tok, d_model), F32),
        compiler_params=_params(("arbitrary",)),
        name="mix_out",
    )(z, z, z, z, z, z, z, attn, conv_w, w_pk, x, g1)


def _split_bf16(a):
    hi = a.astype(BF16)
    lo = (a - hi.astype(F32)).astype(BF16)
    return hi, lo


def _dot3(a_hi, a_lo, b_hi, b_lo):
    d = functools.partial(jnp.dot, preferred_element_type=F32)
    return d(a_hi, b_hi) + (d(a_hi, b_lo) + d(a_lo, b_hi))


def _top_values(s, count):
    vals = []
    work = s
    for k in range(count):
        m = jnp.max(work, axis=0, keepdims=True)
        vals.append(m)
        if k + 1 < count:
            work = jnp.where(work == m, -jnp.inf, work)
    return vals


def _oddeven_merge_sort_pairs(n):
    pairs = []
    p = 1
    while p < n:
        k = p
        while k >= 1:
            for j in range(k % p, n - k, 2 * k):
                for i in range(min(k, n - j - k)):
                    if (i + j) // (2 * p) == (i + j + k) // (2 * p):
                        pairs.append((i + j, i + j + k))
            k //= 2
        p *= 2
    return pairs


def _bitonic_merge_pairs(n):
    pairs = []
    k = n // 2
    while k >= 1:
        pairs += [(i, i + k) for i in range(n) if i & k == 0]
        k //= 2
    return pairs


def _compare_exchange(w, pairs):
    for i, j in pairs:
        w[i], w[j] = jnp.maximum(w[i], w[j]), jnp.minimum(w[i], w[j])


def _top_sorted(s, count):
    assert s.shape[0] == count * SUBLANES and count & (count - 1) == 0
    w = [s[SUBLANES * i:SUBLANES * (i + 1), :] for i in range(count)]
    _compare_exchange(w, _oddeven_merge_sort_pairs(count))
    merge = _bitonic_merge_pairs(count)
    shift = SUBLANES // 2
    while shift >= 1:
        w = [jnp.maximum(w[i], pltpu.roll(w[count - 1 - i], shift, axis=0)) for i in range(count)]
        _compare_exchange(w, merge)
        shift //= 2
    return w


def _stack_rows(rows):
    idx = lax.broadcasted_iota(jnp.int32, rows[0].shape, 0)
    out = rows[0]
    for k in range(1, len(rows)):
        out = jnp.where(idx == k, rows[k], out)
    return out


def _peer_score_kernel(x_ref, g_ref, sc_ref, sh_ref, wq_ref, kh_ref, kl_ref,
                       ht_ref, t1_ref, s2_ref, tau_ref):
    nk = kh_ref.shape[2]
    h = _modulated_norm(x_ref[...], g_ref[...], sc_ref[0], sh_ref[0])
    ht = h.T
    ht_hi = ht.astype(BF16)
    ht_ref[...] = pltpu.bitcast(ht_hi, jnp.uint32)
    qt = jnp.dot(pltpu.bitcast(wq_ref[...], BF16), ht_hi, preferred_element_type=F32)
    half = kh_ref.shape[3]
    k = PEER_TOPK
    for hd in range(PEER_HEADS):
        st = []
        for c in range(2):
            r0 = (hd * 2 + c) * half
            q_hi, q_lo = _split_bf16(qt[r0:r0 + half, :])
            st.append(_dot3(kh_ref[hd, c], kl_ref[hd, c], q_hi, q_lo))
        assert k == 2 * SUBLANES
        a = _top_sorted(st[0], k)
        b = _top_sorted(st[1], k)
        b_lo, b_hi = _stack_rows(b[:k // 2]), _stack_rows(b[k // 2:])
        a_tail = _stack_rows(a[k // 2:])
        cand = [a[0] + b_lo, a[0] + b_hi]
        cand += [a[p] + b_lo for p in range(1, k // 2)]
        cand.append(a_tail + b[0])
        cand = jnp.concatenate(cand, axis=0)
        top = _top_values(cand, k + 1)
        tau = 0.5 * (top[k - 1] + top[k])
        z = jnp.sum(jnp.where(cand > tau, jnp.exp(cand - top[0]), 0.0), axis=0, keepdims=True)
        off = top[0] + jnp.log(z)
        t1_ref[hd * nk:(hd + 1) * nk, :] = (st[0] - off) * LOG2E
        s2_ref[hd * nk:(hd + 1) * nk, :] = st[1] * LOG2E
        tau_ref[hd:hd + 1, :] = (tau - off) * LOG2E


def _peer_score(x, norm_g, sc, sh, wqt_pk, keys_hi, keys_lo, seq):
    n_tok, d_model = x.shape
    nk = keys_hi.shape[2]
    tm = _tile(seq, 256)
    tiles_per_seq = seq // tm
    per_batch = lambda i: (i // tiles_per_seq, 0, 0)
    once = pl.Buffered(1)
    col = lambda i: (0, i)
    return pl.pallas_call(
        _peer_score_kernel,
        grid=(n_tok // tm,),
        in_specs=[
            pl.BlockSpec((tm, d_model), lambda i: (i, 0)),
            pl.BlockSpec((1, d_model), lambda i: (0, 0)),
            pl.BlockSpec((1, 1, d_model), per_batch),
            pl.BlockSpec((1, 1, d_model), per_batch),
            pl.BlockSpec(wqt_pk.shape, lambda i: (0, 0), pipeline_mode=once),
            pl.BlockSpec(keys_hi.shape, lambda i: (0, 0, 0, 0), pipeline_mode=once),
            pl.BlockSpec(keys_lo.shape, lambda i: (0, 0, 0, 0), pipeline_mode=once),
        ],
        out_specs=[
            pl.BlockSpec((d_model // 2, tm), col),
            pl.BlockSpec((PEER_HEADS * nk, tm), col),
            pl.BlockSpec((PEER_HEADS * nk, tm), col),
            pl.BlockSpec((PEER_HEADS, tm), col),
        ],
        out_shape=[
            jax.ShapeDtypeStruct((d_model // 2, n_tok), jnp.uint32),
            jax.ShapeDtypeStruct((PEER_HEADS * nk, n_tok), F32),
            jax.ShapeDtypeStruct((PEER_HEADS * nk, n_tok), F32),
            jax.ShapeDtypeStruct((PEER_HEADS, n_tok), F32),
        ],
        compiler_params=_params(("arbitrary",)),
        name="peer_score",
    )(x, norm_g.reshape(1, d_model), sc, sh, wqt_pk, keys_hi, keys_lo)


GATE_ROWS = 32
PIPE_COLS = 256
LOG2E = 1.0 / math.log(2.0)


def _peer_main_kernel(nk, u_ref, vt_ref, ht_ref, *refs):
    t1_refs = refs[:PEER_HEADS]
    s2_ref, tau_ref, o_ref, a_scr, ag_scr = refs[PEER_HEADS:]
    e = pl.program_id(1)
    eb, tm = a_scr.shape
    d_model = o_ref.shape[0]
    nb = eb // nk
    groups = GATE_ROWS // SUBLANES
    n_chunks = tm // PIPE_COLS
    pieces = d_model // MXU_DIM
    b_rows = eb // pieces
    c_rows = d_model // pieces
    assert nk % b_rows == 0 and b_rows % GATE_ROWS == 0

    @pl.when(e == 0)
    def _():
        o_ref[...] = jnp.zeros_like(o_ref)

    def wide(cc):
        return slice(cc * PIPE_COLS, (cc + 1) * PIPE_COLS)

    def stage_a(cc, p, acc):
        ks = slice(p * MXU_DIM, (p + 1) * MXU_DIM)
        part = jnp.dot(pltpu.bitcast(u_ref[:, ks], BF16), pltpu.bitcast(ht_ref[p * MXU_DIM // 2:(p + 1) * MXU_DIM // 2, wide(cc)], BF16),
                       preferred_element_type=F32)
        acc = part if acc is None else acc + part
        if p == pieces - 1:
            a_scr[:, wide(cc)] = acc
        return acc

    def stage_b(cc, p):
        ii, r_base = divmod(p * b_rows, nk)
        for sub in range(PIPE_COLS // LANES):
            cols = slice(cc * PIPE_COLS + sub * LANES, cc * PIPE_COLS + (sub + 1) * LANES)
            for r0 in range(r_base, r_base + b_rows, GATE_ROWS):
                gate = [None] * groups
                for hd in range(PEER_HEADS):
                    tau_b = jnp.broadcast_to(tau_ref[hd:hd + 1, cols], (SUBLANES, LANES))
                    t1_b = jnp.broadcast_to(t1_refs[hd][ii:ii + 1, cols], (SUBLANES, LANES))
                    for g in range(groups):
                        row = hd * nk + r0 + g * SUBLANES
                        sm = t1_b + s2_ref[row:row + SUBLANES, cols]
                        w = jnp.where(sm > tau_b, jnp.exp2(sm), 0.0)
                        gate[g] = w if gate[g] is None else gate[g] + w
                for g in range(groups):
                    rows = slice(ii * nk + r0 + g * SUBLANES, ii * nk + r0 + (g + 1) * SUBLANES)
                    act = a_scr[rows, cols]
                    gel = 0.5 * act * (1.0 + lax.erf(act * (1.0 / math.sqrt(2.0))))
                    ag_scr[rows, cols] = (gel * gate[g]).astype(BF16)

    def stage_c(cc, p):
        rows = slice(p * c_rows, (p + 1) * c_rows)
        vt_rows = pltpu.bitcast(vt_ref[p * c_rows // 2:(p + 1) * c_rows // 2, :], BF16)
        o_ref[rows, wide(cc)] += jnp.dot(vt_rows, ag_scr[:, wide(cc)], preferred_element_type=F32)

    for slot in range(n_chunks + 2):
        acc = None
        for p in range(pieces):
            if slot < n_chunks:
                acc = stage_a(slot, p, acc)
            if 0 <= slot - 1 < n_chunks:
                stage_b(slot - 1, p)
            if 0 <= slot - 2 < n_chunks:
                stage_c(slot - 2, p)


def _pack_kernel(transpose, x_ref, o_ref):
    x = x_ref[0]
    if transpose:
        x = x.T
    o_ref[...] = pltpu.bitcast(x.astype(BF16), jnp.uint32)


def _pack_weight(w_all, layer, transpose=False):
    _, r, n = w_all.shape
    br = _tile(r, 512)
    if transpose:
        out_spec = pl.BlockSpec((n // 2, br), lambda i: (0, i))
        out_shape = jax.ShapeDtypeStruct((n // 2, r), jnp.uint32)
    else:
        out_spec = pl.BlockSpec((br // 2, n), lambda i: (i, 0))
        out_shape = jax.ShapeDtypeStruct((r // 2, n), jnp.uint32)
    return pl.pallas_call(
        functools.partial(_pack_kernel, transpose),
        grid=(r // br,),
        in_specs=[pl.BlockSpec((1, br, n), lambda i: (layer, i, 0))],
        out_specs=out_spec,
        out_shape=out_shape,
        compiler_params=_params(("arbitrary",)),
        name="pack_weight",
    )(w_all)


def _peer_main(u_pk, vt_pk, ht, t1, s2, tau, nk):
    n_exp, d_model = 2 * u_pk.shape[0], u_pk.shape[1]
    n_tok = ht.shape[1]
    tm = _tile(n_tok, 1024)
    eb = _tile(n_exp, 1024)
    n_blocks = n_exp // eb
    once = pl.Buffered(1)
    col = lambda i, e: (0, i)
    nb = eb // nk
    t1_specs = [pl.BlockSpec((nb, tm), functools.partial(lambda hd, i, e: (hd * n_blocks + e, i), hd))
                for hd in range(PEER_HEADS)]
    return pl.pallas_call(
        functools.partial(_peer_main_kernel, nk),
        grid=(n_tok // tm, n_blocks),
        in_specs=[
            pl.BlockSpec((eb // 2, d_model), lambda i, e: (e, 0)),
            pl.BlockSpec((d_model // 2, eb), lambda i, e: (0, e)),
            pl.BlockSpec((d_model // 2, tm), col, pipeline_mode=once),
            *t1_specs,
            pl.BlockSpec((s2.shape[0], tm), col, pipeline_mode=once),
            pl.BlockSpec((tau.shape[0], tm), col, pipeline_mode=once),
        ],
        out_specs=pl.BlockSpec((d_model, tm), col),
        out_shape=jax.ShapeDtypeStruct((d_model, n_tok), F32),
        scratch_shapes=[pltpu.VMEM((eb, tm), F32), pltpu.VMEM((eb, tm), BF16)],
        compiler_params=_params(("arbitrary", "arbitrary")),
        name="peer_main",
    )(u_pk, vt_pk, ht, *([t1] * PEER_HEADS), s2, tau)


def _final_kernel(x_ref, yt_ref, g2_ref, g_ref, o_ref):
    x = x_ref[...] + g2_ref[0] * yt_ref[...].T
    ms = jnp.mean(x * x, axis=-1, keepdims=True)
    o_ref[...] = (x * lax.rsqrt(ms + NORM_EPS)) * g_ref[...]


def _final(x, yt, g2, final_g, seq):
    n_tok, d_model = x.shape
    tm = _tile(seq, 512)
    tiles_per_seq = seq // tm
    return pl.pallas_call(
        _final_kernel,
        grid=(n_tok // tm,),
        in_specs=[
            pl.BlockSpec((tm, d_model), lambda i: (i, 0)),
            pl.BlockSpec((d_model, tm), lambda i: (0, i)),
            pl.BlockSpec((1, 1, d_model), lambda i: (i // tiles_per_seq, 0, 0)),
            pl.BlockSpec((1, d_model), lambda i: (0, 0)),
        ],
        out_specs=pl.BlockSpec((tm, d_model), lambda i: (i, 0)),
        out_shape=jax.ShapeDtypeStruct((n_tok, d_model), F32),
        compiler_params=_params(("arbitrary",)),
        name="final_norm",
    )(x, yt, g2, final_g.reshape(1, d_model))


def kernel(x, c, positions, ada_w, ada_b, norm1_g, w_in, conv_w, lambda_q1, lambda_k1, lambda_q2, lambda_k2,
           subln_g, w_out, norm2_g, peer_wq, peer_keys, peer_u, peer_v, final_g):
    bsz, seq, d_model = x.shape
    depth = ada_w.shape[0]
    n_tok = bsz * seq
    conv_cols = conv_w.shape[2]
    attn_cols = w_out.shape[1] - conv_cols
    q_cols = 3 * conv_cols
    k_cols = q_cols + attn_cols
    v_cols = k_cols + attn_cols
    nk = peer_keys.shape[3]
    sub_dim = attn_cols // ATTN_HEADS // 2
    q_scale = sub_dim ** -0.5 * LOG2E

    rope = _rope_tables(positions)
    mod = _ada_modulation(c, ada_w, ada_b)
    xf = x.reshape(n_tok, d_model)
    prev = None
    for l in range(depth):
        lam_init = 0.8 - 0.6 * math.exp(-0.3 * l)
        sh1, sc1, g1, sh2, sc2, g2 = [m.reshape(bsz, 1, d_model) for m in jnp.split(mod[l], 6, axis=-1)]
        z, xf = _mix_in(xf, prev, norm1_g[l], sc1, sh1, _pack_weight(w_in, l), rope, seq,
                        q_cols, k_cols, v_cols, q_scale)
        lam_vecs = jnp.stack([lambda_q1[l], lambda_k1[l], lambda_q2[l], lambda_k2[l]])
        attn = _attention(z, lam_vecs, subln_g[l], bsz, seq, q_cols, k_cols, v_cols, lam_init)
        xf = _mix_out(z, attn, conv_w[l], _pack_weight(w_out, l), xf, g1, seq, conv_cols)
        wqt_pk = _pack_weight(peer_wq, l, transpose=True)
        keys_hi, keys_lo = _split_bf16(peer_keys[l])
        ht, t1, s2, tau = _peer_score(xf, norm2_g[l], sc2, sh2, wqt_pk, keys_hi, keys_lo, seq)
        yt = _peer_main(_pack_weight(peer_u, l), _pack_weight(peer_v, l, transpose=True), ht, t1, s2, tau, nk)
        prev = (yt, g2)
    out = _final(xf, prev[0], prev[1], final_g, seq)
    return out.reshape(bsz, seq, d_model)
```

```python
import functools
import math

import jax
import jax.numpy as jnp
from jax import lax
from jax.experimental import pallas as pl
from jax.experimental.pallas import tpu as pltpu

F32 = jnp.float32
BF16 = jnp.bfloat16

NORM_EPS = 1e-6
ROPE_THETA = 500000.0
ATTN_HEADS = 8
ROPE_DIM = 16
PEER_HEADS = 8
PEER_TOPK = 16
CONV_K = 3

MXU_DIM = 256
LANES = 128
SUBLANES = 8
VMEM_LIMIT = 56 * 1024 * 1024


def _tile(n, pref):
    return pref if n % pref == 0 else n


def _params(sem):
    return pltpu.CompilerParams(dimension_semantics=sem, vmem_limit_bytes=VMEM_LIMIT)


def _ada_kernel(cb_ref, w_ref, b_ref, o_ref, act_scr):
    d_model = w_ref.shape[1]
    tn = w_ref.shape[2]
    bsz = cb_ref.shape[1]
    groups = tn // LANES

    @pl.when((pl.program_id(0) == 0) & (pl.program_id(1) == 0))
    def _():
        cv = cb_ref[...]
        act_scr[...] = cv / (1.0 + jnp.exp(-cv))

    def body(d8, acc):
        acc = list(acc)
        for r in range(SUBLANES):
            d = d8 * SUBLANES + r
            cv = act_scr[d]
            wrow = w_ref[0, pl.ds(d, 1), :]
            for g in range(groups):
                acc[g] = acc[g] + cv * wrow[:, g * LANES:(g + 1) * LANES]
        return tuple(acc)

    acc0 = tuple(jnp.zeros((bsz, LANES), F32) for _ in range(groups))
    acc = lax.fori_loop(0, d_model // SUBLANES, body, acc0)
    for g in range(groups):
        o_ref[0, :, g * LANES:(g + 1) * LANES] = acc[g] + b_ref[0, :, g * LANES:(g + 1) * LANES]


def _ada_modulation(c, ada_w, ada_b):
    depth, d_model, n_out = ada_w.shape
    bsz = c.shape[0]
    tn = _tile(n_out, 1024)
    cb = jnp.broadcast_to(c.T[:, :, None], (d_model, bsz, LANES))
    return pl.pallas_call(
        _ada_kernel,
        grid=(depth, n_out // tn),
        in_specs=[
            pl.BlockSpec((d_model, bsz, LANES), lambda l, j: (0, 0, 0), pipeline_mode=pl.Buffered(1)),
            pl.BlockSpec((1, d_model, tn), lambda l, j: (l, 0, j)),
            pl.BlockSpec((1, 1, tn), lambda l, j: (l, 0, j)),
        ],
        out_specs=pl.BlockSpec((1, bsz, tn), lambda l, j: (l, 0, j)),
        out_shape=jax.ShapeDtypeStruct((depth, bsz, n_out), F32),
        scratch_shapes=[pltpu.VMEM((d_model, bsz, LANES), F32)],
        compiler_params=_params(("arbitrary", "arbitrary")),
        name="ada_modulation",
    )(cb, ada_w, ada_b.reshape(depth, 1, n_out))


def _rope_table_kernel(pos_ref, freq_ref, c_ref, s1_ref, s2_ref):
    ang = pos_ref[...].astype(F32) * freq_ref[...]
    lane = lax.broadcasted_iota(jnp.int32, ang.shape, 1) % (LANES // 2)
    half = ROPE_DIM // 2
    cosv = jnp.cos(ang)
    sinv = jnp.sin(ang)
    c_ref[...] = jnp.where(lane < ROPE_DIM, cosv, 1.0)
    s1_ref[...] = jnp.where((lane >= half) & (lane < ROPE_DIM), sinv, 0.0)
    s2_ref[...] = jnp.where(lane < half, -sinv, 0.0)


def _rope_tables(positions):
    n_tok = positions.size
    half = ROPE_DIM // 2
    inv_freq = ROPE_THETA ** (-jnp.arange(0, ROPE_DIM, 2, dtype=F32) / ROPE_DIM)
    sub = jnp.concatenate([inv_freq, inv_freq, jnp.zeros((LANES // 2 - 2 * half,), F32)])
    freq_row = jnp.concatenate([sub, sub]).reshape(1, LANES)
    tm = _tile(n_tok, 2048)
    tab = jax.ShapeDtypeStruct((n_tok, LANES), F32)
    return pl.pallas_call(
        _rope_table_kernel,
        grid=(n_tok // tm,),
        in_specs=[pl.BlockSpec((tm, 1), lambda i: (i, 0)), pl.BlockSpec((1, LANES), lambda i: (0, 0))],
        out_specs=[pl.BlockSpec((tm, LANES), lambda i: (i, 0))] * 3,
        out_shape=[tab, tab, tab],
        compiler_params=_params(("arbitrary",)),
        name="rope_tables",
    )(positions.reshape(n_tok, 1), freq_row)


def _modulated_norm(x, g, sc, sh):
    ms = jnp.mean(x * x, axis=-1, keepdims=True)
    return (x * lax.rsqrt(ms + NORM_EPS)) * g * (1.0 + sc) + sh


def _mix_in_kernel(has_prev, rope_lo, rope_mid, rope_hi, q_scale, *refs):
    if has_prev:
        (x_ref, yt_ref, g2_ref, g_ref, sc_ref, sh_ref, w_ref, rc_ref, rs1_ref, rs2_ref,
         z_ref, xo_ref, h_scr) = refs
    else:
        (x_ref, g_ref, sc_ref, sh_ref, w_ref, rc_ref, rs1_ref, rs2_ref, z_ref, h_scr) = refs
    j = pl.program_id(1)

    @pl.when(j == 0)
    def _():
        x = x_ref[...]
        if has_prev:
            x = x + g2_ref[0] * yt_ref[...].T
            xo_ref[...] = x
        h_scr[...] = _modulated_norm(x, g_ref[...], sc_ref[0], sh_ref[0]).astype(BF16)

    z = jnp.dot(h_scr[...], pltpu.bitcast(w_ref[...], BF16), preferred_element_type=F32)
    is_rope = (j >= rope_lo) & (j < rope_hi)

    @pl.when(jnp.logical_not(is_rope))
    def _():
        z_ref[...] = z.astype(BF16)

    @pl.when(is_rope)
    def _():
        reps = z.shape[1] // LANES
        scale = jnp.where(j < rope_mid, q_scale, 1.0).astype(F32)
        zr_prev = pltpu.roll(z, ROPE_DIM // 2, axis=1)
        zr_next = pltpu.roll(z, z.shape[1] - ROPE_DIM // 2, axis=1)
        for r in range(reps):
            sl = slice(r * LANES, (r + 1) * LANES)
            out = z[:, sl] * rc_ref[...] + zr_prev[:, sl] * rs1_ref[...] + zr_next[:, sl] * rs2_ref[...]
            z_ref[:, sl] = (out * scale).astype(BF16)


def _mix_in(x, prev, norm_g, sc, sh, w_pk, rope, seq, q_cols, k_cols, v_cols, q_scale):
    n_tok, d_model = x.shape
    n_out = w_pk.shape[1]
    tm = _tile(seq, 512)
    tn = _tile(q_cols, 1024)
    assert q_cols % tn == 0 and k_cols % tn == 0 and v_cols % tn == 0
    tiles_per_seq = seq // tm
    has_prev = prev is not None
    row = lambda i, j: (i, 0)
    per_batch = lambda i, j: (i // tiles_per_seq, 0, 0)
    fixed = lambda i, j: (0, 0)
    in_specs = [pl.BlockSpec((tm, d_model), row)]
    args = [x]
    if has_prev:
        yt, g2 = prev
        in_specs += [pl.BlockSpec((d_model, tm), lambda i, j: (0, i)), pl.BlockSpec((1, 1, d_model), per_batch)]
        args += [yt, g2]
    in_specs += [
        pl.BlockSpec((1, d_model), fixed),
        pl.BlockSpec((1, 1, d_model), per_batch),
        pl.BlockSpec((1, 1, d_model), per_batch),
        pl.BlockSpec((d_model // 2, tn), lambda i, j: (0, j)),
        pl.BlockSpec((tm, LANES), row),
        pl.BlockSpec((tm, LANES), row),
        pl.BlockSpec((tm, LANES), row),
    ]
    args += [norm_g.reshape(1, d_model), sc, sh, w_pk, *rope]
    out_specs = [pl.BlockSpec((tm, tn), lambda i, j: (i, j))]
    out_shape = [jax.ShapeDtypeStruct((n_tok, n_out), BF16)]
    if has_prev:
        out_specs.append(pl.BlockSpec((tm, d_model), row))
        out_shape.append(jax.ShapeDtypeStruct((n_tok, d_model), F32))
    kern = functools.partial(_mix_in_kernel, has_prev, q_cols // tn, k_cols // tn, v_cols // tn, q_scale)
    outs = pl.pallas_call(
        kern,
        grid=(n_tok // tm, n_out // tn),
        in_specs=in_specs,
        out_specs=out_specs,
        out_shape=out_shape,
        scratch_shapes=[pltpu.VMEM((tm, d_model), BF16)],
        compiler_params=_params(("arbitrary", "arbitrary")),
        name="mix_in",
    )(*args)
    return (outs[0], outs[1]) if has_prev else (outs[0], x)


def _attention_kernel(lam_init, tq, q_ref, k_ref, v_ref, lam_ref, g_ref, o_ref):
    def sub_block(i, carry):
        rows = pl.ds(pl.multiple_of(i * tq, tq), tq)
        _attention_rows(lam_init, tq, q_ref[rows, :], k_ref, v_ref, lam_ref, g_ref, o_ref, rows)
        return carry

    lax.fori_loop(0, q_ref.shape[0] // tq, sub_block, 0)


def _attention_rows(lam_init, tq, q, k_ref, v_ref, lam_ref, g_ref, o_ref, rows):
    lane = lax.broadcasted_iota(jnp.int32, q.shape, 1)
    zero = jnp.zeros_like(q)
    qq = jnp.concatenate([jnp.where(lane < LANES // 2, q, zero), jnp.where(lane >= LANES // 2, q, zero)], axis=0)
    s = lax.dot_general(qq, k_ref[...], (((1,), (1,)), ((), ())), preferred_element_type=F32)
    m = jnp.max(s, axis=1, keepdims=True)
    p = jnp.exp2(s - m)
    l = jnp.sum(p, axis=1, keepdims=True)
    lv = lam_ref[...]
    lam = (jnp.exp(jnp.sum(lv[0:1] * lv[1:2], axis=1, keepdims=True))
           - jnp.exp(jnp.sum(lv[2:3] * lv[3:4], axis=1, keepdims=True)) + lam_init)
    r = lam * l[:tq] / l[tq:]
    a = (p[:tq] - p[tq:] * r).astype(BF16)
    o = jnp.dot(a, v_ref[...], preferred_element_type=F32) * (1.0 / l[:tq])
    ms = jnp.mean(o * o, axis=-1, keepdims=True)
    o_ref[rows, :] = ((o * lax.rsqrt(ms + NORM_EPS)) * g_ref[...] * (1.0 - lam_init)).astype(BF16)


def _attention(z, lam_vecs, subln_g, bsz, seq, q_cols, k_cols, v_cols, lam_init):
    n_tok = z.shape[0]
    hd = LANES
    tq_sub = _tile(seq, 256)
    tq = _tile(seq, 4 * tq_sub)
    nq = seq // tq
    qb, kb, vb = q_cols // hd, k_cols // hd, v_cols // hd
    return pl.pallas_call(
        functools.partial(_attention_kernel, lam_init, tq_sub),
        grid=(bsz, ATTN_HEADS, nq),
        in_specs=[
            pl.BlockSpec((tq, hd), lambda b, h, i: (b * nq + i, qb + h)),
            pl.BlockSpec((seq, hd), lambda b, h, i: (b, kb + h)),
            pl.BlockSpec((seq, hd), lambda b, h, i: (b, vb + h)),
            pl.BlockSpec(lam_vecs.shape, lambda b, h, i: (0, 0)),
            pl.BlockSpec((1, hd), lambda b, h, i: (0, 0)),
        ],
        out_specs=pl.BlockSpec((tq, hd), lambda b, h, i: (b * nq + i, h)),
        out_shape=jax.ShapeDtypeStruct((n_tok, ATTN_HEADS * hd), BF16),
        compiler_params=_params(("arbitrary", "arbitrary", "arbitrary")),
        name="diff_attention",
    )(z, z, z, lam_vecs, subln_g.reshape(1, hd))


def _mix_out_kernel(tiles_per_seq, gb_ref, gc_ref, hv_ref, gcp_ref, hvp_ref, gcn_ref, hvn_ref, at_ref,
                    cw_ref, w_ref, x_ref, g1_ref, o_ref):
    i = pl.program_id(0)
    tm = gc_ref.shape[0]
    cw = gc_ref.shape[1]
    zc = gc_ref[...].astype(F32) * hv_ref[...].astype(F32)
    first = (i % tiles_per_seq) == 0
    last = (i % tiles_per_seq) == tiles_per_seq - 1
    prev_row = gcp_ref[SUBLANES - 1:SUBLANES, :].astype(F32) * hvp_ref[SUBLANES - 1:SUBLANES, :].astype(F32)
    next_row = gcn_ref[0:1, :].astype(F32) * hvn_ref[0:1, :].astype(F32)
    prev_row = jnp.where(first, 0.0, prev_row)
    next_row = jnp.where(last, 0.0, next_row)
    rows = lax.broadcasted_iota(jnp.int32, zc.shape, 0)
    z_prev = jnp.where(rows == 0, prev_row, pltpu.roll(zc, 1, axis=0))
    z_next = jnp.where(rows == tm - 1, next_row, pltpu.roll(zc, tm - 1, axis=0))
    y = cw_ref[0:1, :] * z_prev + cw_ref[1:2, :] * zc + cw_ref[2:3, :] * z_next
    conv = (gb_ref[...].astype(F32) * y).astype(BF16)
    acc = jnp.dot(conv, pltpu.bitcast(w_ref[:cw // 2, :], BF16), preferred_element_type=F32)
    acc = acc + jnp.dot(at_ref[...], pltpu.bitcast(w_ref[cw // 2:, :], BF16), preferred_element_type=F32)
    o_ref[...] = x_ref[...] + g1_ref[0] * acc


def _mix_out(z, attn, conv_w, w_pk, x, g1, seq, conv_cols):
    n_tok, d_model = x.shape
    cw = conv_cols
    aw = attn.shape[1]
    tm = _tile(seq, 512)
    tiles_per_seq = seq // tm
    hb = tm // SUBLANES
    last_hb = n_tok // SUBLANES - 1
    row = lambda i: (i, 0)
    prev_blk = lambda i: (jnp.maximum(i * hb - 1, 0), 1)
    prev_blk2 = lambda i: (jnp.maximum(i * hb - 1, 0), 2)
    next_blk = lambda i: (jnp.minimum((i + 1) * hb, last_hb), 1)
    next_blk2 = lambda i: (jnp.minimum((i + 1) * hb, last_hb), 2)
    return pl.pallas_call(
        functools.partial(_mix_out_kernel, tiles_per_seq),
        grid=(n_tok // tm,),
        in_specs=[
            pl.BlockSpec((tm, cw), lambda i: (i, 0)),
            pl.BlockSpec((tm, cw), lambda i: (i, 1)),
            pl.BlockSpec((tm, cw), lambda i: (i, 2)),
            pl.BlockSpec((SUBLANES, cw), prev_blk),
            pl.BlockSpec((SUBLANES, cw), prev_blk2),
            pl.BlockSpec((SUBLANES, cw), next_blk),
            pl.BlockSpec((SUBLANES, cw), next_blk2),
            pl.BlockSpec((tm, aw), row),
            pl.BlockSpec((CONV_K, cw), lambda i: (0, 0)),
            pl.BlockSpec(((cw + aw) // 2, d_model), lambda i: (0, 0), pipeline_mode=pl.Buffered(1)),
            pl.BlockSpec((tm, d_model), row),
            pl.BlockSpec((1, 1, d_model), lambda i: (i // tiles_per_seq, 0, 0)),
        ],
        out_specs=pl.BlockSpec((tm, d_model), row),
        out_shape=jax.ShapeDtypeStruct((n_tok, d_model), F32),
        compiler_params=_params(("arbitrary",)),
        name="mix_out",
    )(z, z, z, z, z, z, z, attn, conv_w, w_pk, x, g1)


def _split_bf16(a):
    hi = a.astype(BF16)
    lo = (a - hi.astype(F32)).astype(BF16)
    return hi, lo


def _dot3(a_hi, a_lo, b_hi, b_lo):
    d = functools.partial(jnp.dot, preferred_element_type=F32)
    return d(a_hi, b_hi) + (d(a_hi, b_lo) + d(a_lo, b_hi))


def _top_values(s, count):
    vals = []
    work = s
    for k in range(count):
        m = jnp.max(work, axis=0, keepdims=True)
        vals.append(m)
        if k + 1 < count:
            work = jnp.where(work == m, -jnp.inf, work)
    return vals


def _oddeven_merge_sort_pairs(n):
    pairs = []
    p = 1
    while p < n:
        k = p
        while k >= 1:
            for j in range(k % p, n - k, 2 * k):
                for i in range(min(k, n - j - k)):
                    if (i + j) // (2 * p) == (i + j + k) // (2 * p):
                        pairs.append((i + j, i + j + k))
            k //= 2
        p *= 2
    return pairs


def _bitonic_merge_pairs(n):
    pairs = []
    k = n // 2
    while k >= 1:
        pairs += [(i, i + k) for i in range(n) if i & k == 0]
        k //= 2
    return pairs


def _compare_exchange(w, pairs):
    for i, j in pairs:
        w[i], w[j] = jnp.maximum(w[i], w[j]), jnp.minimum(w[i], w[j])


def _top_sorted(s, count):
    assert s.shape[0] == count * SUBLANES and count & (count - 1) == 0
    w = [s[SUBLANES * i:SUBLANES * (i + 1), :] for i in range(count)]
    _compare_exchange(w, _oddeven_merge_sort_pairs(count))
    merge = _bitonic_merge_pairs(count)
    shift = SUBLANES // 2
    while shift >= 1:
        w = [jnp.maximum(w[i], pltpu.roll(w[count - 1 - i], shift, axis=0)) for i in range(count)]
        _compare_exchange(w, merge)
        shift //= 2
    return w


def _stack_rows(rows):
    idx = lax.broadcasted_iota(jnp.int32, rows[0].shape, 0)
    out = rows[0]
    for k in range(1, len(rows)):
        out = jnp.where(idx == k, rows[k], out)
    return out


def _peer_score_kernel(x_ref, g_ref, sc_ref, sh_ref, wq_ref, kh_ref, kl_ref,
                       ht_ref, t1_ref, s2_ref, tau_ref):
    nk = kh_ref.shape[2]
    h = _modulated_norm(x_ref[...], g_ref[...], sc_ref[0], sh_ref[0])
    ht = h.T
    ht_hi = ht.astype(BF16)
    ht_ref[...] = pltpu.bitcast(ht_hi, jnp.uint32)
    qt = jnp.dot(pltpu.bitcast(wq_ref[...], BF16), ht_hi, preferred_element_type=F32)
    half = kh_ref.shape[3]
    k = PEER_TOPK
    for hd in range(PEER_HEADS):
        st = []
        for c in range(2):
            r0 = (hd * 2 + c) * half
            q_hi, q_lo = _split_bf16(qt[r0:r0 + half, :])
            st.append(_dot3(kh_ref[hd, c], kl_ref[hd, c], q_hi, q_lo))
        assert k == 2 * SUBLANES
        a = _top_sorted(st[0], k)
        b = _top_sorted(st[1], k)
        b_lo, b_hi = _stack_rows(b[:k // 2]), _stack_rows(b[k // 2:])
        a_tail = _stack_rows(a[k // 2:])
        cand = [a[0] + b_lo, a[0] + b_hi]
        cand += [a[p] + b_lo for p in range(1, k // 2)]
        cand.append(a_tail + b[0])
        cand = jnp.concatenate(cand, axis=0)
        top = _top_values(cand, k + 1)
        tau = 0.5 * (top[k - 1] + top[k])
        z = jnp.sum(jnp.where(cand > tau, jnp.exp(cand - top[0]), 0.0), axis=0, keepdims=True)
        off = top[0] + jnp.log(z)
        t1_ref[hd * nk:(hd + 1) * nk, :] = (st[0] - off) * LOG2E
        s2_ref[hd * nk:(hd + 1) * nk, :] = st[1] * LOG2E
        tau_ref[hd:hd + 1, :] = (tau - off) * LOG2E


def _peer_score(x, norm_g, sc, sh, wqt_pk, keys_hi, keys_lo, seq):
    n_tok, d_model = x.shape
    nk = keys_hi.shape[2]
    tm = _tile(seq, 256)
    tiles_per_seq = seq // tm
    per_batch = lambda i: (i // tiles_per_seq, 0, 0)
    once = pl.Buffered(1)
    col = lambda i: (0, i)
    return pl.pallas_call(
        _peer_score_kernel,
        grid=(n_tok // tm,),
        in_specs=[
            pl.BlockSpec((tm, d_model), lambda i: (i, 0)),
            pl.BlockSpec((1, d_model), lambda i: (0, 0)),
            pl.BlockSpec((1, 1, d_model), per_batch),
            pl.BlockSpec((1, 1, d_model), per_batch),
            pl.BlockSpec(wqt_pk.shape, lambda i: (0, 0), pipeline_mode=once),
            pl.BlockSpec(keys_hi.shape, lambda i: (0, 0, 0, 0), pipeline_mode=once),
            pl.BlockSpec(keys_lo.shape, lambda i: (0, 0, 0, 0), pipeline_mode=once),
        ],
        out_specs=[
            pl.BlockSpec((d_model // 2, tm), col),
            pl.BlockSpec((PEER_HEADS * nk, tm), col),
            pl.BlockSpec((PEER_HEADS * nk, tm), col),
            pl.BlockSpec((PEER_HEADS, tm), col),
        ],
        out_shape=[
            jax.ShapeDtypeStruct((d_model // 2, n_tok), jnp.uint32),
            jax.ShapeDtypeStruct((PEER_HEADS * nk, n_tok), F32),
            jax.ShapeDtypeStruct((PEER_HEADS * nk, n_tok), F32),
            jax.ShapeDtypeStruct((PEER_HEADS, n_tok), F32),
        ],
        compiler_params=_params(("arbitrary",)),
        name="peer_score",
    )(x, norm_g.reshape(1, d_model), sc, sh, wqt_pk, keys_hi, keys_lo)


GATE_ROWS = 32
PIPE_COLS = 256
LOG2E = 1.0 / math.log(2.0)


def _peer_main_kernel(nk, u_ref, vt_ref, ht_ref, *refs):
    t1_refs = refs[:PEER_HEADS]
    s2_ref, tau_ref, o_ref, a_scr, ag_scr = refs[PEER_HEADS:]
    e = pl.program_id(1)
    eb, tm = a_scr.shape
    d_model = o_ref.shape[0]
    nb = eb // nk
    groups = GATE_ROWS // SUBLANES
    n_chunks = tm // PIPE_COLS
    pieces = d_model // MXU_DIM
    b_rows = eb // pieces
    c_rows = d_model // pieces
    assert nk % b_rows == 0 and b_rows % GATE_ROWS == 0

    @pl.when(e == 0)
    def _():
        o_ref[...] = jnp.zeros_like(o_ref)

    def wide(cc):
        return slice(cc * PIPE_COLS, (cc + 1) * PIPE_COLS)

    def stage_a(cc, p, acc):
        ks = slice(p * MXU_DIM, (p + 1) * MXU_DIM)
        part = jnp.dot(pltpu.bitcast(u_ref[:, ks], BF16), pltpu.bitcast(ht_ref[p * MXU_DIM // 2:(p + 1) * MXU_DIM // 2, wide(cc)], BF16),
                       preferred_element_type=F32)
        acc = part if acc is None else acc + part
        if p == pieces - 1:
            a_scr[:, wide(cc)] = acc
        return acc

    def stage_b(cc, p):
        ii, r_base = divmod(p * b_rows, nk)
        for sub in range(PIPE_COLS // LANES):
            cols = slice(cc * PIPE_COLS + sub * LANES, cc * PIPE_COLS + (sub + 1) * LANES)
            for r0 in range(r_base, r_base + b_rows, GATE_ROWS):
                gate = [None] * groups
                for hd in range(PEER_HEADS):
                    tau_b = jnp.broadcast_to(tau_ref[hd:hd + 1, cols], (SUBLANES, LANES))
                    t1_b = jnp.broadcast_to(t1_refs[hd][ii:ii + 1, cols], (SUBLANES, LANES))
                    for g in range(groups):
                        row = hd * nk + r0 + g * SUBLANES
                        sm = t1_b + s2_ref[row:row + SUBLANES, cols]
                        w = jnp.where(sm > tau_b, jnp.exp2(sm), 0.0)
                        gate[g] = w if gate[g] is None else gate[g] + w
                for g in range(groups):
                    rows = slice(ii * nk + r0 + g * SUBLANES, ii * nk + r0 + (g + 1) * SUBLANES)
                    act = a_scr[rows, cols]
                    gel = 0.5 * act * (1.0 + lax.erf(act * (1.0 / math.sqrt(2.0))))
                    ag_scr[rows, cols] = (gel * gate[g]).astype(BF16)

    def stage_c(cc, p):
        rows = slice(p * c_rows, (p + 1) * c_rows)
        vt_rows = pltpu.bitcast(vt_ref[p * c_rows // 2:(p + 1) * c_rows // 2, :], BF16)
        o_ref[rows, wide(cc)] += jnp.dot(vt_rows, ag_scr[:, wide(cc)], preferred_element_type=F32)

    for slot in range(n_chunks + 2):
        acc = None
        for p in range(pieces):
            if slot < n_chunks:
                acc = stage_a(slot, p, acc)
            if 0 <= slot - 1 < n_chunks:
                stage_b(slot - 1, p)
            if 0 <= slot - 2 < n_chunks:
                stage_c(slot - 2, p)


def _pack_kernel(transpose, x_ref, o_ref):
    x = x_ref[0]
    if transpose:
        x = x.T
    o_ref[...] = pltpu.bitcast(x.astype(BF16), jnp.uint32)


def _pack_weight(w_all, layer, transpose=False):
    _, r, n = w_all.shape
    br = _tile(r, 512)
    if transpose:
        out_spec = pl.BlockSpec((n // 2, br), lambda i: (0, i))
        out_shape = jax.ShapeDtypeStruct((n // 2, r), jnp.uint32)
    else:
        out_spec = pl.BlockSpec((br // 2, n), lambda i: (i, 0))
        out_shape = jax.ShapeDtypeStruct((r // 2, n), jnp.uint32)
    return pl.pallas_call(
        functools.partial(_pack_kernel, transpose),
        grid=(r // br,),
        in_specs=[pl.BlockSpec((1, br, n), lambda i: (layer, i, 0))],
        out_specs=out_spec,
        out_shape=out_shape,
        compiler_params=_params(("arbitrary",)),
        name="pack_weight",
    )(w_all)


def _peer_main(u_pk, vt_pk, ht, t1, s2, tau, nk):
    n_exp, d_model = 2 * u_pk.shape[0], u_pk.shape[1]
    n_tok = ht.shape[1]
    tm = _tile(n_tok, 1024)
    eb = _tile(n_exp, 1024)
    n_blocks = n_exp // eb
    once = pl.Buffered(1)
    col = lambda i, e: (0, i)
    nb = eb // nk
    t1_specs = [pl.BlockSpec((nb, tm), functools.partial(lambda hd, i, e: (hd * n_blocks + e, i), hd))
                for hd in range(PEER_HEADS)]
    return pl.pallas_call(
        functools.partial(_peer_main_kernel, nk),
        grid=(n_tok // tm, n_blocks),
        in_specs=[
            pl.BlockSpec((eb // 2, d_model), lambda i, e: (e, 0)),
            pl.BlockSpec((d_model // 2, eb), lambda i, e: (0, e)),
            pl.BlockSpec((d_model // 2, tm), col, pipeline_mode=once),
            *t1_specs,
            pl.BlockSpec((s2.shape[0], tm), col, pipeline_mode=once),
            pl.BlockSpec((tau.shape[0], tm), col, pipeline_mode=once),
        ],
        out_specs=pl.BlockSpec((d_model, tm), col),
        out_shape=jax.ShapeDtypeStruct((d_model, n_tok), F32),
        scratch_shapes=[pltpu.VMEM((eb, tm), F32), pltpu.VMEM((eb, tm), BF16)],
        compiler_params=_params(("arbitrary", "arbitrary")),
        name="peer_main",
    )(u_pk, vt_pk, ht, *([t1] * PEER_HEADS), s2, tau)


def _final_kernel(x_ref, yt_ref, g2_ref, g_ref, o_ref):
    x = x_ref[...] + g2_ref[0] * yt_ref[...].T
    ms = jnp.mean(x * x, axis=-1, keepdims=True)
    o_ref[...] = (x * lax.rsqrt(ms + NORM_EPS)) * g_ref[...]


def _final(x, yt, g2, final_g, seq):
    n_tok, d_model = x.shape
    tm = _tile(seq, 512)
    tiles_per_seq = seq // tm
    return pl.pallas_call(
        _final_kernel,
        grid=(n_tok // tm,),
        in_specs=[
            pl.BlockSpec((tm, d_model), lambda i: (i, 0)),
            pl.BlockSpec((d_model, tm), lambda i: (0, i)),
            pl.BlockSpec((1, 1, d_model), lambda i: (i // tiles_per_seq, 0, 0)),
            pl.BlockSpec((1, d_model), lambda i: (0, 0)),
        ],
        out_specs=pl.BlockSpec((tm, d_model), lambda i: (i, 0)),
        out_shape=jax.ShapeDtypeStruct((n_tok, d_model), F32),
        compiler_params=_params(("arbitrary",)),
        name="final_norm",
    )(x, yt, g2, final_g.reshape(1, d_model))


def kernel(x, c, positions, ada_w, ada_b, norm1_g, w_in, conv_w, lambda_q1, lambda_k1, lambda_q2, lambda_k2,
           subln_g, w_out, norm2_g, peer_wq, peer_keys, peer_u, peer_v, final_g):
    bsz, seq, d_model = x.shape
    depth = ada_w.shape[0]
    n_tok = bsz * seq
    conv_cols = conv_w.shape[2]
    attn_cols = w_out.shape[1] - conv_cols
    q_cols = 3 * conv_cols
    k_cols = q_cols + attn_cols
    v_cols = k_cols + attn_cols
    nk = peer_keys.shape[3]
    sub_dim = attn_cols // ATTN_HEADS // 2
    q_scale = sub_dim ** -0.5 * LOG2E

    rope = _rope_tables(positions)
    mod = _ada_modulation(c, ada_w, ada_b)
    xf = x.reshape(n_tok, d_model)
    prev = None
    for l in range(depth):
        lam_init = 0.8 - 0.6 * math.exp(-0.3 * l)
        sh1, sc1, g1, sh2, sc2, g2 = [m.reshape(bsz, 1, d_model) for m in jnp.split(mod[l], 6, axis=-1)]
        z, xf = _mix_in(xf, prev, norm1_g[l], sc1, sh1, _pack_weight(w_in, l), rope, seq,
                        q_cols, k_cols, v_cols, q_scale)
        lam_vecs = jnp.stack([lambda_q1[l], lambda_k1[l], lambda_q2[l], lambda_k2[l]])
        attn = _attention(z, lam_vecs, subln_g[l], bsz, seq, q_cols, k_cols, v_cols, lam_init)
        xf = _mix_out(z, attn, conv_w[l], _pack_weight(w_out, l), xf, g1, seq, conv_cols)
        wqt_pk = _pack_weight(peer_wq, l, transpose=True)
        keys_hi, keys_lo = _split_bf16(peer_keys[l])
        ht, t1, s2, tau = _peer_score(xf, norm2_g[l], sc2, sh2, wqt_pk, keys_hi, keys_lo, seq)
        yt = _peer_main(_pack_weight(peer_u, l), _pack_weight(peer_v, l, transpose=True), ht, t1, s2, tau, nk)
        prev = (yt, g2)
    out = _final(xf, prev[0], prev[1], final_g, seq)
    return out.reshape(bsz, seq, d_model)
```
